```python
import math
import jax, jax.numpy as jnp
from jax import lax
import numpy as np

D_MODEL = 1024
BATCH = 16
SEQ = 2048
DEPTH = 2

CHUNK = 64
HEAD_DIM = 64
M_HEADS = 4
R_HEADS = 6
W_HEADS = 6
M_WIDTH = M_HEADS * HEAD_DIM
R_WIDTH = R_HEADS * HEAD_DIM
W_WIDTH = W_HEADS * HEAD_DIM
MIX_WIDTH = M_WIDTH + R_WIDTH + W_WIDTH
CONV_WIDTH = 4
W_LORA = 64
A_LORA = 64
G_LORA = 128
D_FF = -(-8 * D_MODEL // (3 * 256)) * 256
M_COLS = 4 * M_WIDTH + 2 * M_HEADS
R_COLS = 4 * R_WIDTH
W_COLS = 3 * W_WIDTH + W_LORA + A_LORA + G_LORA
P_COLS = M_COLS + R_COLS + W_COLS
RMS_EPS = 1e-6
HEAD_LN_EPS = 1e-5
RWKV_LN_EPS = 64e-5
ROPE_BASE = 10000.0
M_INIT = -1e30

kernel_name = "hybrid_mlstm_retention_rwkv7_trunk"


def rmsnorm(x, g):
    xf = x.astype(jnp.float32)
    y = xf * lax.rsqrt(jnp.mean(xf * xf, axis=-1, keepdims=True) + RMS_EPS) * g.astype(jnp.float32)
    return y.astype(x.dtype)


def head_layernorm(x, n_heads, eps):
    B, T, C = x.shape
    xh = x.astype(jnp.float32).reshape(B, T, n_heads, C // n_heads)
    mu = jnp.mean(xh, axis=-1, keepdims=True)
    var = jnp.mean(jnp.square(xh - mu), axis=-1, keepdims=True)
    return ((xh - mu) * lax.rsqrt(var + eps)).reshape(B, T, C)


def causal_depthwise_conv(x, w):
    K, C = w.shape
    return lax.conv_general_dilated(
        x, w.astype(x.dtype)[:, None, :], window_strides=(1,), padding=[(K - 1, 0)],
        dimension_numbers=("NWC", "WIO", "NWC"), feature_group_count=C)


def to_chunks(x, n_heads):
    B, T, C = x.shape
    return x.reshape(B, T // CHUNK, CHUNK, n_heads, C // n_heads).transpose(1, 0, 3, 2, 4)


def from_chunks(y):
    NC, B, H, L, d = y.shape
    return y.transpose(1, 0, 3, 2, 4).reshape(B, NC * L, H * d)


def gate_chunks(g):
    B, T, H = g.shape
    return g.reshape(B, T // CHUNK, CHUNK, H).transpose(1, 0, 3, 2)


def mlstm_chunkwise(q, k, v, i_pre, log_f):
    B, T, _ = q.shape
    H, d = M_HEADS, HEAD_DIM
    causal = jnp.tril(jnp.ones((CHUNK, CHUNK), dtype=bool))

    def step(carry, inp):
        C, n, m = carry
        qc, kc, vc, ic, lfc = inp
        b = jnp.cumsum(lfc, axis=-1)
        dlog = jnp.where(causal, b[..., :, None] - b[..., None, :] + ic[..., None, :], -jnp.inf)
        inter = b + m[..., None]
        m_row = jnp.maximum(inter, jnp.max(dlog, axis=-1))
        s = jnp.einsum('bhjd,bhsd->bhjs', qc, kc) * jnp.exp(dlog - m_row[..., None])
        inter_w = jnp.exp(inter - m_row)
        num = jnp.einsum('bhjs,bhse->bhje', s, vc) + inter_w[..., None] * jnp.einsum('bhjd,bhde->bhje', qc, C)
        den = jnp.sum(s, axis=-1) + inter_w * jnp.einsum('bhjd,bhd->bhj', qc, n)
        h = num / jnp.maximum(jnp.abs(den), jnp.exp(-m_row))[..., None]
        bL = b[..., -1]
        src = bL[..., None] - b + ic
        m_new = jnp.maximum(bL + m, jnp.max(src, axis=-1))
        sw = jnp.exp(src - m_new[..., None])
        dec = jnp.exp(bL + m - m_new)
        C_new = dec[..., None, None] * C + jnp.einsum('bhs,bhsd,bhse->bhde', sw, kc, vc)
        n_new = dec[..., None] * n + jnp.einsum('bhs,bhsd->bhd', sw, kc)
        return (C_new, n_new, m_new), h

    init = (jnp.zeros((B, H, d, d), jnp.float32), jnp.zeros((B, H, d), jnp.float32),
            jnp.full((B, H), M_INIT, jnp.float32))
    xs = (to_chunks(q, H), to_chunks(k, H), to_chunks(v, H), gate_chunks(i_pre), gate_chunks(log_f))
    _, h = lax.scan(step, init, xs)
    return from_chunks(h)


def mlstm_group(p, conv_w, b_i, b_f, ln_g):
    p = p.astype(jnp.float32)
    qk, v, o, gi, gf = jnp.split(p, [2 * M_WIDTH, 3 * M_WIDTH, 4 * M_WIDTH, 4 * M_WIDTH + M_HEADS], axis=-1)
    qk = jax.nn.silu(causal_depthwise_conv(qk, conv_w))
    q, k = jnp.split(qk, 2, axis=-1)
    i_pre = gi + b_i.astype(jnp.float32)
    log_f = jax.nn.log_sigmoid(gf + b_f.astype(jnp.float32))
    h = mlstm_chunkwise(q, k * HEAD_DIM ** -0.5, v, i_pre, log_f)
    h = head_layernorm(h, M_HEADS, HEAD_LN_EPS) * ln_g.astype(jnp.float32)
    return jax.nn.sigmoid(o) * h


def rotary(x, cos, sin):
    x1, x2 = jnp.split(x, 2, axis=-1)
    c, s = cos[None, :, None, :], sin[None, :, None, :]
    return jnp.concatenate([x1 * c - x2 * s, x1 * s + x2 * c], axis=-1)


def retention_group(p):
    p = p.astype(jnp.float32)
    B, T, _ = p.shape
    H, d = R_HEADS, HEAD_DIM
    q, k, v, g = jnp.split(p, 4, axis=-1)
    pos = jnp.arange(T, dtype=jnp.float32)
    theta = 1.0 / (ROPE_BASE ** jnp.linspace(0.0, 1.0, d // 2, dtype=jnp.float32))
    ang = pos[:, None] * theta[None, :]
    cos, sin = jnp.cos(ang), jnp.sin(ang)
    q = rotary(q.reshape(B, T, H, d), cos, sin).reshape(B, T, H * d)
    k = rotary(k.reshape(B, T, H, d), cos, sin).reshape(B, T, H * d) * d ** -0.5
    log_g = jnp.log(1.0 - 2.0 ** (-5.0 - jnp.arange(H, dtype=jnp.float32)))
    idx = jnp.arange(CHUNK, dtype=jnp.float32)
    diff = idx[:, None] - idx[None, :]
    dmat = jnp.where(diff >= 0, jnp.exp(jnp.maximum(diff, 0.0)[None] * log_g[:, None, None]), 0.0)
    q_dec = jnp.exp((idx[None, :] + 1.0) * log_g[:, None])[None, :, :, None]
    k_dec = jnp.exp((CHUNK - 1.0 - idx[None, :]) * log_g[:, None])[None, :, :, None]
    c_dec = jnp.exp(CHUNK * log_g)[None, :, None, None]

    def step(R, inp):
        qc, kc, vc = inp
        s = jnp.einsum('bhjd,bhsd->bhjs', qc, kc) * dmat
        out = jnp.einsum('bhjs,bhse->bhje', s, vc) + jnp.einsum('bhjd,bhde->bhje', qc, R) * q_dec
        R_new = c_dec * R + jnp.einsum('bhsd,bhse->bhde', kc * k_dec, vc)
        return R_new, out

    R0 = jnp.zeros((B, H, d, d), jnp.float32)
    _, y = lax.scan(step, R0, (to_chunks(q, H), to_chunks(k, H), to_chunks(v, H)))
    y = head_layernorm(from_chunks(y), H, HEAD_LN_EPS)
    return jax.nn.silu(g) * y


def rwkv7_group(p, mu, w0, w_up, a0, a_up, g_up, k_k, k_a, r_k, ln_g, ln_b):
    p = p.astype(jnp.float32)
    B, T, _ = p.shape
    H, d = W_HEADS, HEAD_DIM
    p_prev = jnp.pad(p, ((0, 0), (1, 0), (0, 0)))[:, :-1]
    p = p + mu.astype(jnp.float32) * (p_prev - p)
    r, k, v, wl, al, gl = jnp.split(
        p, [W_WIDTH, 2 * W_WIDTH, 3 * W_WIDTH, 3 * W_WIDTH + W_LORA, 3 * W_WIDTH + W_LORA + A_LORA], axis=-1)
    w_log = -jax.nn.softplus(-(w0 + jnp.tanh(wl) @ w_up.astype(jnp.float32))) - 0.5
    decay = jnp.exp(-jnp.exp(w_log))
    a = jax.nn.sigmoid(a0 + al @ a_up.astype(jnp.float32))
    g = jax.nn.sigmoid(gl) @ g_up.astype(jnp.float32)
    kk = (k * k_k).reshape(B, T, H, d)
    kk = kk / jnp.maximum(jnp.sqrt(jnp.sum(kk * kk, axis=-1, keepdims=True)), 1e-12)
    k = k * (1.0 + (a - 1.0) * k_a)

    def heads_t(z):
        return z.reshape(B, T, H, d).transpose(1, 0, 2, 3)

    def step(S, inp):
        r_t, w_t, k_t, v_t, kk_t, a_t = inp
        S = (S * w_t[:, :, None, :]
             - jnp.einsum('bhvk,bhk->bhv', S, kk_t)[..., None] * (kk_t * a_t)[:, :, None, :]
             + v_t[..., :, None] * k_t[..., None, :])
        return S, jnp.einsum('bhvk,bhk->bhv', S, r_t)

    S0 = jnp.zeros((B, H, d, d), jnp.float32)
    xs = (heads_t(r), heads_t(decay), heads_t(k), heads_t(v), kk.transpose(1, 0, 2, 3), heads_t(a))
    _, y = lax.scan(step, S0, xs)
    y = y.transpose(1, 0, 2, 3).reshape(B, T, H * d)
    y = head_layernorm(y, H, RWKV_LN_EPS) * ln_g + ln_b
    bonus = jnp.sum(r.reshape(B, T, H, d) * k.reshape(B, T, H, d) * r_k.reshape(H, d), axis=-1, keepdims=True)
    y = y + (bonus * v.reshape(B, T, H, d)).reshape(B, T, H * d)
    return y * g


def setup_inputs(seed: int = 0) -> dict:
    key = jax.random.key(seed)
    ks = jax.random.split(key, 24)
    f32 = jnp.float32
    nrm = lambda k, shape, s: jax.random.normal(k, shape, f32) * s
    w0_base = jnp.linspace(-5.5, 0.5, W_WIDTH, dtype=f32)
    bf_base = jnp.linspace(3.0, 6.0, M_HEADS, dtype=f32)
    return {
        "x": nrm(ks[0], (BATCH, SEQ, D_MODEL), 1.0),
        "w_in": nrm(ks[1], (DEPTH, D_MODEL, P_COLS), D_MODEL ** -0.5),
        "ln1_g": 1.0 + nrm(ks[2], (DEPTH, D_MODEL), 0.02),
        "ln2_g": 1.0 + nrm(ks[3], (DEPTH, D_MODEL), 0.02),
        "lnf_g": 1.0 + nrm(ks[4], (D_MODEL,), 0.02),
        "m_conv": nrm(ks[5], (DEPTH, CONV_WIDTH, 2 * M_WIDTH), CONV_WIDTH ** -0.5),
        "m_b_i": nrm(ks[6], (DEPTH, M_HEADS), 0.1),
        "m_b_f": bf_base + nrm(ks[7], (DEPTH, M_HEADS), 0.1),
        "m_ln_g": 1.0 + nrm(ks[8], (DEPTH, M_WIDTH), 0.02),
        "rw_mu": jax.random.uniform(ks[9], (DEPTH, W_COLS), f32),
        "rw_w0": w0_base + nrm(ks[10], (DEPTH, W_WIDTH), 0.1),
        "rw_w_up": nrm(ks[11], (DEPTH, W_LORA, W_WIDTH), 0.1),
        "rw_a0": nrm(ks[12], (DEPTH, W_WIDTH), 0.1),
        "rw_a_up": nrm(ks[13], (DEPTH, A_LORA, W_WIDTH), 0.1),
        "rw_g_up": nrm(ks[14], (DEPTH, G_LORA, W_WIDTH), G_LORA ** -0.5),
        "rw_k_k": 0.85 + nrm(ks[15], (DEPTH, W_WIDTH), 0.02),
        "rw_k_a": 1.0 + nrm(ks[16], (DEPTH, W_WIDTH), 0.02),
        "rw_r_k": nrm(ks[17], (DEPTH, W_WIDTH), 0.1),
        "rw_ln_g": 1.0 + nrm(ks[18], (DEPTH, W_WIDTH), 0.02),
        "rw_ln_b": nrm(ks[19], (DEPTH, W_WIDTH), 0.01),
        "w_out": nrm(ks[20], (DEPTH, MIX_WIDTH, D_MODEL), MIX_WIDTH ** -0.5),
        "w_gate": nrm(ks[21], (DEPTH, D_MODEL, D_FF), D_MODEL ** -0.5),
        "w_up": nrm(ks[22], (DEPTH, D_MODEL, D_FF), D_MODEL ** -0.5),
        "w_down": nrm(ks[23], (DEPTH, D_FF, D_MODEL), D_FF ** -0.5),
    }


def reference(x, w_in, ln1_g, ln2_g, lnf_g, m_conv, m_b_i, m_b_f, m_ln_g, rw_mu, rw_w0, rw_w_up,
              rw_a0, rw_a_up, rw_g_up, rw_k_k, rw_k_a, rw_r_k, rw_ln_g, rw_ln_b, w_out, w_gate,
              w_up, w_down):
    for l in range(DEPTH):
        h = rmsnorm(x, ln1_g[l])
        p = h @ w_in[l]
        pm, pr, pw = jnp.split(p, [M_COLS, M_COLS + R_COLS], axis=-1)
        ym = mlstm_group(pm, m_conv[l], m_b_i[l], m_b_f[l], m_ln_g[l])
        yr = retention_group(pr)
        yw = rwkv7_group(pw, rw_mu[l], rw_w0[l], rw_w_up[l], rw_a0[l], rw_a_up[l], rw_g_up[l],
                         rw_k_k[l], rw_k_a[l], rw_r_k[l], rw_ln_g[l], rw_ln_b[l])
        y = jnp.concatenate([ym, yr, yw], axis=-1).astype(x.dtype)
        x = x + y @ w_out[l]
        h = rmsnorm(x, ln2_g[l])
        x = x + (jax.nn.silu(h @ w_gate[l]) * (h @ w_up[l])) @ w_down[l]
    return rmsnorm(x, lnf_g)
```

```python
import functools
import math

import jax
import jax.numpy as jnp
from jax import lax
from jax.experimental import pallas as pl
from jax.experimental.pallas import tpu as pltpu

F32 = jnp.float32
BF16 = jnp.bfloat16

CHUNK = 64
HEAD_DIM = 64
M_HEADS, R_HEADS, W_HEADS = 4, 6, 6
M_WIDTH, R_WIDTH, W_WIDTH = 256, 384, 384
W_LORA, A_LORA, G_LORA = 64, 64, 128
M_COLS = 4 * M_WIDTH + 2 * M_HEADS
R_COLS = 4 * R_WIDTH
W_COLS = 3 * W_WIDTH + W_LORA + A_LORA + G_LORA
PM_W = 4 * M_WIDTH + 128
P_W = PM_W + R_COLS + W_COLS
RMS_EPS = 1e-6
HEAD_LN_EPS = 1e-5
RWKV_LN_EPS = 64e-5
ROPE_BASE = 10000.0
M_INIT = -1e30
VMEM_LIMIT = 56 * 1024 * 1024


def _bf(x):
    return x.astype(BF16)


def _dot(a, b):
    return jnp.dot(_bf(a), _bf(b), preferred_element_type=F32)


def _dot_nt(a, b):
    return lax.dot_general(_bf(a), _bf(b), (((1,), (1,)), ((), ())), preferred_element_type=F32)


def _dot_tn(a, b):
    return lax.dot_general(_bf(a), _bf(b), (((0,), (0,)), ((), ())), preferred_element_type=F32)


def _split2(x):
    hi = _bf(x)
    lo = _bf(x - hi.astype(F32))
    return hi, lo


def _dot3(a, b):
    ah, al = _split2(a)
    bh, bl = _split2(b)
    return (jnp.dot(ah, bh, preferred_element_type=F32)
            + jnp.dot(ah, bl, preferred_element_type=F32)
            + jnp.dot(al, bh, preferred_element_type=F32))


def _split3(x):
    x1 = _bf(x)
    r = x - x1.astype(F32)
    x2 = _bf(r)
    x3 = _bf(r - x2.astype(F32))
    return x1, x2, x3


def _dot_exact_rhs(m_bf, x):
    x1, x2, x3 = _split3(x)
    return (jnp.dot(m_bf, x1, preferred_element_type=F32)
            + jnp.dot(m_bf, x2, preferred_element_type=F32)
            + jnp.dot(m_bf, x3, preferred_element_type=F32))


def _dot_exact_lhs(x, m_bf):
    x1, x2, x3 = _split3(x)
    return (jnp.dot(x1, m_bf, preferred_element_type=F32)
            + jnp.dot(x2, m_bf, preferred_element_type=F32)
            + jnp.dot(x3, m_bf, preferred_element_type=F32))


def _sigmoid(x):
    return 1.0 / (1.0 + jnp.exp(-x))


def _silu(x):
    return x * _sigmoid(x)


def _softplus(x):
    return jnp.maximum(x, 0.0) + jnp.log(1.0 + jnp.exp(-jnp.abs(x)))


def _log_sigmoid(x):
    return -_softplus(-x)


def _rms(x, g):
    return x * lax.rsqrt(jnp.mean(x * x, axis=-1, keepdims=True) + RMS_EPS) * g


def _head_ln(x, eps):
    mu = jnp.mean(x, axis=-1, keepdims=True)
    xc = x - mu
    var = jnp.mean(xc * xc, axis=-1, keepdims=True)
    return xc * lax.rsqrt(var + eps)


def _shift_rows(cur, prev, j):
    row = lax.broadcasted_iota(jnp.int32, cur.shape, 0)
    return jnp.where(row >= j, pltpu.roll(cur, j, 0), pltpu.roll(prev, j, 0))


def _rot_half(x):
    pieces = []
    for i in range(x.shape[1] // 128):
        p = x[:, 128 * i:128 * (i + 1)]
        lane = lax.broadcasted_iota(jnp.int32, p.shape, 1)
        pieces.append(jnp.where((lane & 63) < 32, pltpu.roll(p, 96, 1), pltpu.roll(p, 32, 1)))
    return jnp.concatenate(pieces, axis=1)


def _in_kernel(x_ref, g_ref, w_ref, pm_ref, pr_ref, pw_ref):
    hb = _bf(_rms(x_ref[...], g_ref[...]))
    pm_ref[...] = jnp.dot(hb, w_ref[:, 0:PM_W], preferred_element_type=F32)
    pr_ref[...] = jnp.dot(hb, w_ref[:, PM_W:PM_W + R_COLS], preferred_element_type=F32)
    pw_ref[...] = jnp.dot(hb, w_ref[:, PM_W + R_COLS:P_W], preferred_element_type=F32)


def _in_proj(x2, g, w_pad, tm):
    n, d = x2.shape
    return pl.pallas_call(
        _in_kernel,
        grid=(n // tm,),
        in_specs=[pl.BlockSpec((tm, d), lambda i: (i, 0)),
                  pl.BlockSpec((1, d), lambda i: (0, 0)),
                  pl.BlockSpec((d, P_W), lambda i: (0, 0))],
        out_specs=[pl.BlockSpec((tm, PM_W), lambda i: (i, 0)),
                   pl.BlockSpec((tm, R_COLS), lambda i: (i, 0)),
                   pl.BlockSpec((tm, W_COLS), lambda i: (i, 0))],
        out_shape=[jax.ShapeDtypeStruct((n, PM_W), F32),
                   jax.ShapeDtypeStruct((n, R_COLS), F32),
                   jax.ShapeDtypeStruct((n, W_COLS), F32)],
        compiler_params=pltpu.CompilerParams(dimension_semantics=("arbitrary",),
                                             vmem_limit_bytes=VMEM_LIMIT),
        name="in_proj",
    )(x2, g, w_pad)


def _mlstm_chunk(pm, gt, prev_qk, conv_w, b_i_row, b_f_row, b_i_col, b_f_col, ln_g,
                 c_scr, n_scr, m_scr, tril_bf, triu_bf, causal):
    qk_raw = pm[:, 0:2 * M_WIDTH]
    acc = qk_raw * conv_w[3:4, :]
    for j in (1, 2, 3):
        acc = acc + _shift_rows(qk_raw, prev_qk, j) * conv_w[3 - j:4 - j, :]
    qk = _silu(acc)
    q = qk[:, 0:M_WIDTH]
    k = qk[:, M_WIDTH:2 * M_WIDTH] * (HEAD_DIM ** -0.5)
    v = pm[:, 2 * M_WIDTH:3 * M_WIDTH]
    o = pm[:, 3 * M_WIDTH:4 * M_WIDTH]
    gates = pm[:, 4 * M_WIDTH:4 * M_WIDTH + 2 * M_HEADS]
    i_col = gates[:, 0:M_HEADS] + b_i_row
    lf_col = _log_sigmoid(gates[:, M_HEADS:2 * M_HEADS] + b_f_row)
    i_row = gt[0:M_HEADS, :] + b_i_col
    lf_row = _log_sigmoid(gt[M_HEADS:2 * M_HEADS, :] + b_f_col)
    b_col = _dot_exact_rhs(tril_bf, lf_col)
    b_row = _dot_exact_lhs(lf_row, triu_bf)
    outs = []
    for h in range(M_HEADS):
        sl = slice(HEAD_DIM * h, HEAD_DIM * (h + 1))
        qh, kh, vh = q[:, sl], k[:, sl], v[:, sl]
        bc = b_col[:, h:h + 1]
        ic = i_col[:, h:h + 1]
        m_old = m_scr[h]
        dlog = jnp.where(causal, bc - b_row[h:h + 1, :] + i_row[h:h + 1, :], -jnp.inf)
        inter = bc + m_old
        m_row = jnp.maximum(inter, jnp.max(dlog, axis=1, keepdims=True))
        s = _dot_nt(qh, kh) * jnp.exp(dlog - m_row)
        inter_w = jnp.exp(inter - m_row)
        c_old = c_scr[h]
        n_old = n_scr[h]
        num = _dot(s, vh) + inter_w * _dot(qh, c_old)
        den = jnp.sum(s, axis=1, keepdims=True) + inter_w * jnp.sum(qh * n_old, axis=1, keepdims=True)
        hh = num / jnp.maximum(jnp.abs(den), jnp.exp(-m_row))
        b_last = bc[CHUNK - 1:CHUNK, :]
        src = b_last - bc + ic
        m_new = jnp.maximum(b_last + m_old, jnp.max(src, axis=0, keepdims=True))
        sw = jnp.exp(src - m_new)
        dec = jnp.exp(b_last + m_old - m_new)
        ksw = kh * sw
        c_scr[h] = dec * c_old + _dot_tn(ksw, vh)
        n_scr[h] = dec * n_old + jnp.sum(ksw, axis=0, keepdims=True)
        m_scr[h] = m_new
        outs.append(_sigmoid(o[:, sl]) * (_head_ln(hh, HEAD_LN_EPS) * ln_g[:, sl]))
    return outs, qk_raw


def _retention_chunk(pr, cos, sin, dmat_ref, qdec, kdec, cdec, r_scr):
    q = pr[:, 0:R_WIDTH]
    k = pr[:, R_WIDTH:2 * R_WIDTH]
    v = pr[:, 2 * R_WIDTH:3 * R_WIDTH]
    g = pr[:, 3 * R_WIDTH:4 * R_WIDTH]
    qr = q * cos + _rot_half(q) * sin
    kr = (k * cos + _rot_half(k) * sin) * (HEAD_DIM ** -0.5)
    qd = qr * qdec
    kd = kr * kdec
    outs = []
    for h in range(R_HEADS):
        sl = slice(HEAD_DIM * h, HEAD_DIM * (h + 1))
        s = _dot_nt(qr[:, sl], kr[:, sl]) * dmat_ref[h]
        r_old = r_scr[h]
        out = _dot(s, v[:, sl]) + _dot(qd[:, sl], r_old)
        r_scr[h] = cdec[:, sl] * r_old + _dot_tn(kd[:, sl], v[:, sl])
        outs.append(_silu(g[:, sl]) * _head_ln(out, HEAD_LN_EPS))
    return outs


def _rwkv_chunk(pw, prev_pw, mu, w0, w_up, a0, a_up, g_up, k_k, k_a, r_k, ln_g, ln_b,
                s_scr, tril_bf, lower_incl2, lower_strict2, eye):
    ps = pw + mu * (_shift_rows(pw, prev_pw, 1) - pw)
    r = ps[:, 0:W_WIDTH]
    k = ps[:, W_WIDTH:2 * W_WIDTH]
    v = ps[:, 2 * W_WIDTH:3 * W_WIDTH]
    o1 = 3 * W_WIDTH
    wl = ps[:, o1:o1 + W_LORA]
    al = ps[:, o1 + W_LORA:o1 + W_LORA + A_LORA]
    gl = ps[:, o1 + W_LORA + A_LORA:o1 + W_LORA + A_LORA + G_LORA]
    w_log = -_softplus(-(w0 + _dot3(jnp.tanh(wl), w_up))) - 0.5
    lw = -jnp.exp(w_log)
    a = _sigmoid(a0 + _dot3(al, a_up))
    g = _dot(_sigmoid(gl), g_up)
    kk0 = k * k_k
    k2 = k * (1.0 + (a - 1.0) * k_a)
    cum = _dot_exact_rhs(tril_bf, lw)
    e_pos = jnp.exp(cum)
    e_neg = jnp.exp(-cum)
    e_last = jnp.exp(cum[CHUNK - 1:CHUNK, :] - cum)
    g_last = e_pos[CHUNK - 1:CHUNK, :]
    rk = r * k2 * r_k
    outs = []
    for h in range(W_HEADS):
        sl = slice(HEAD_DIM * h, HEAD_DIM * (h + 1))
        kk0h = kk0[:, sl]
        nrm = jnp.sqrt(jnp.sum(kk0h * kk0h, axis=1, keepdims=True))
        kkh = kk0h / jnp.maximum(nrm, 1e-12)
        ah = a[:, sl]
        vh = v[:, sl]
        a_t = kkh * jnp.exp(cum[:, sl] - lw[:, sl])
        b_t = -(kkh * ah)
        ar = jnp.concatenate([a_t, r[:, sl] * e_pos[:, sl]], axis=0)
        bk = jnp.concatenate([b_t * e_neg[:, sl], k2[:, sl] * e_neg[:, sl]], axis=0)
        bk_end = jnp.concatenate([b_t * e_last[:, sl], k2[:, sl] * e_last[:, sl]], axis=0)
        m = _dot_nt(ar, bk)
        top = jnp.where(lower_strict2, m[0:CHUNK, :], 0.0)
        bot = jnp.where(lower_incl2, m[CHUNK:2 * CHUNK, :], 0.0)
        p = top[:, 0:CHUNK]
        t_inv = eye + p
        for _ in range(5):
            p = _dot3(p, p)
            t_inv = t_inv + _dot3(t_inv, p)
        s_old = s_scr[h]
        gs = _dot_nt(ar, s_old)
        u = _dot3(t_inv, gs[0:CHUNK, :] + _dot(top[:, CHUNK:2 * CHUNK], vh))
        uv = jnp.concatenate([u, vh], axis=0)
        y = gs[CHUNK:2 * CHUNK, :] + _dot(bot, uv)
        s_scr[h] = s_old * g_last[:, sl] + _dot_tn(uv, bk_end)
        y = _head_ln(y, RWKV_LN_EPS) * ln_g[:, sl] + ln_b[:, sl]
        bonus = jnp.sum(rk[:, sl], axis=1, keepdims=True)
        outs.append((y + bonus * vh) * g[:, sl])
    return outs


def _mix_kernel(pm_ref, pr_ref, pw_ref, gt_ref, cos_ref, sin_ref, dmat_ref, qdec_ref, kdec_ref,
                cdec_ref, conv_ref, bi_row_ref, bf_row_ref, bi_col_ref, bf_col_ref, mln_ref,
                mu_ref, w0_ref, wup_ref, a0_ref, aup_ref, gup_ref, kk_ref, ka_ref, rk_ref,
                wlng_ref, wlnb_ref,
                y_ref,
                c_scr, n_scr, m_scr, r_scr, s_scr, prevqk_scr, prevpw_scr):
    @pl.when(pl.program_id(1) == 0)
    def _():
        c_scr[...] = jnp.zeros(c_scr.shape, F32)
        n_scr[...] = jnp.zeros(n_scr.shape, F32)
        m_scr[...] = jnp.full(m_scr.shape, M_INIT, F32)
        r_scr[...] = jnp.zeros(r_scr.shape, F32)
        s_scr[...] = jnp.zeros(s_scr.shape, F32)
        prevqk_scr[...] = jnp.zeros(prevqk_scr.shape, F32)
        prevpw_scr[...] = jnp.zeros(prevpw_scr.shape, F32)

    row = lax.broadcasted_iota(jnp.int32, (CHUNK, CHUNK), 0)
    col = lax.broadcasted_iota(jnp.int32, (CHUNK, CHUNK), 1)
    causal = col <= row
    tril_bf = jnp.where(causal, 1.0, 0.0).astype(BF16)
    triu_bf = jnp.where(row <= col, 1.0, 0.0).astype(BF16)
    eye = jnp.where(row == col, 1.0, 0.0).astype(F32)
    row2 = lax.broadcasted_iota(jnp.int32, (CHUNK, 2 * CHUNK), 0)
    col2 = lax.broadcasted_iota(jnp.int32, (CHUNK, 2 * CHUNK), 1) & (CHUNK - 1)
    lower_incl2 = col2 <= row2
    lower_strict2 = col2 < row2

    pm = pm_ref[0]
    m_outs, qk_raw = _mlstm_chunk(
        pm, gt_ref[0, 0], prevqk_scr[...], conv_ref[...], bi_row_ref[...], bf_row_ref[...],
        bi_col_ref[...], bf_col_ref[...], mln_ref[...], c_scr, n_scr, m_scr, tril_bf, triu_bf, causal)
    prevqk_scr[...] = qk_raw

    r_outs = _retention_chunk(pr_ref[0], cos_ref[...], sin_ref[...], dmat_ref, qdec_ref[...],
                              kdec_ref[...], cdec_ref[...], r_scr)

    pw = pw_ref[0]
    w_outs = _rwkv_chunk(pw, prevpw_scr[...], mu_ref[...], w0_ref[...], wup_ref[...], a0_ref[...],
                         aup_ref[...], gup_ref[...], kk_ref[...], ka_ref[...], rk_ref[...],
                         wlng_ref[...], wlnb_ref[...], s_scr, tril_bf, lower_incl2, lower_strict2, eye)
    prevpw_scr[...] = pw

    y_ref[0] = jnp.concatenate(m_outs + r_outs + w_outs, axis=1).astype(y_ref.dtype)


def _full(shape):
    nd = len(shape)
    return pl.BlockSpec(shape, lambda b, c: (0,) * nd)


def _mixers(pm, pr, pw, gt, tabs, lp):
    bsz, t, _ = pm.shape
    nc = t // CHUNK
    cos, sin, dmat, qdec, kdec, cdec = tabs
    args = [pm, pr, pw, gt, cos, sin, dmat, qdec, kdec, cdec] + list(lp)
    in_specs = [
        pl.BlockSpec((1, CHUNK, PM_W), lambda b, c: (b, c, 0)),
        pl.BlockSpec((1, CHUNK, R_COLS), lambda b, c: (b, c, 0)),
        pl.BlockSpec((1, CHUNK, W_COLS), lambda b, c: (b, c, 0)),
        pl.BlockSpec((1, 1, 2 * M_HEADS, CHUNK), lambda b, c: (b, c, 0, 0)),
        pl.BlockSpec((CHUNK, R_WIDTH), lambda b, c: (c, 0)),
        pl.BlockSpec((CHUNK, R_WIDTH), lambda b, c: (c, 0)),
    ] + [_full(a.shape) for a in args[6:]]
    d_out = M_WIDTH + R_WIDTH + W_WIDTH
    return pl.pallas_call(
        _mix_kernel,
        grid=(bsz, nc),
        in_specs=in_specs,
        out_specs=pl.BlockSpec((1, CHUNK, d_out), lambda b, c: (b, c, 0)),
        out_shape=jax.ShapeDtypeStruct((bsz, t, d_out), BF16),
        scratch_shapes=[
            pltpu.VMEM((M_HEADS, HEAD_DIM, HEAD_DIM), F32),
            pltpu.VMEM((M_HEADS, 1, HEAD_DIM), F32),
            pltpu.VMEM((M_HEADS, 1, 1), F32),
            pltpu.VMEM((R_HEADS, HEAD_DIM, HEAD_DIM), F32),
            pltpu.VMEM((W_HEADS, HEAD_DIM, HEAD_DIM), F32),
            pltpu.VMEM((CHUNK, 2 * M_WIDTH), F32),
            pltpu.VMEM((CHUNK, W_COLS), F32),
        ],
        compiler_params=pltpu.CompilerParams(dimension_semantics=("arbitrary", "arbitrary"),
                                             vmem_limit_bytes=VMEM_LIMIT),
        name="mixers",
    )(*args)


def _out_kernel(y_ref, w_ref, x_ref, o_ref):
    o_ref[...] = x_ref[...] + jnp.dot(y_ref[...], w_ref[...], preferred_element_type=F32)


def _out_proj(y2, w_bf, x2, tm):
    n, d = x2.shape
    return pl.pallas_call(
        _out_kernel,
        grid=(n // tm,),
        in_specs=[pl.BlockSpec((tm, y2.shape[1]), lambda i: (i, 0)),
                  pl.BlockSpec(w_bf.shape, lambda i: (0, 0)),
                  pl.BlockSpec((tm, d), lambda i: (i, 0))],
        out_specs=pl.BlockSpec((tm, d), lambda i: (i, 0)),
        out_shape=jax.ShapeDtypeStruct((n, d), F32),
        compiler_params=pltpu.CompilerParams(dimension_semantics=("arbitrary",),
                                             vmem_limit_bytes=VMEM_LIMIT),
        name="out_proj",
    )(y2, w_bf, x2)


def _ffn_kernel(x_ref, g_ref, wg_ref, wu_ref, wd_ref, gf_ref, o_ref, *, ff_chunk, final_norm):
    x = x_ref[...]
    hb = _bf(_rms(x, g_ref[...]))
    acc = x
    d_ff = wg_ref.shape[1]
    for c in range(d_ff // ff_chunk):
        cs = slice(c * ff_chunk, (c + 1) * ff_chunk)
        gate = jnp.dot(hb, wg_ref[:, cs], preferred_element_type=F32)
        up = jnp.dot(hb, wu_ref[:, cs], preferred_element_type=F32)
        acc = acc + jnp.dot(_bf(_silu(gate) * up), wd_ref[cs, :], preferred_element_type=F32)
    if final_norm:
        acc = _rms(acc, gf_ref[...])
    o_ref[...] = acc


def _ffn(x2, g, wg, wu, wd, gfin, tm, final_norm):
    n, d = x2.shape
    d_ff = wg.shape[1]
    kern = functools.partial(_ffn_kernel, ff_chunk=256, final_norm=final_norm)
    return pl.pallas_call(
        kern,
        grid=(n // tm,),
        in_specs=[pl.BlockSpec((tm, d), lambda i: (i, 0)),
                  pl.BlockSpec((1, d), lambda i: (0, 0)),
                  pl.BlockSpec((d, d_ff), lambda i: (0, 0)),
                  pl.BlockSpec((d, d_ff), lambda i: (0, 0)),
                  pl.BlockSpec((d_ff, d), lambda i: (0, 0)),
                  pl.BlockSpec((1, d), lambda i: (0, 0))],
        out_specs=pl.BlockSpec((tm, d), lambda i: (i, 0)),
        out_shape=jax.ShapeDtypeStruct((n, d), F32),
        compiler_params=pltpu.CompilerParams(dimension_semantics=("arbitrary",),
                                             vmem_limit_bytes=VMEM_LIMIT),
        name="ffn",
    )(x2, g, wg, wu, wd, gfin)


def _retention_tables(t):
    h, d = R_HEADS, HEAD_DIM
    pos = jnp.arange(t, dtype=F32)
    theta = 1.0 / (ROPE_BASE ** jnp.linspace(0.0, 1.0, d // 2, dtype=F32))
    ang = pos[:, None] * theta[None, :]
    cos, sin = jnp.cos(ang), jnp.sin(ang)
    cos_t = jnp.tile(jnp.concatenate([cos, cos], axis=1), (1, h))
    sin_t = jnp.tile(jnp.concatenate([-sin, sin], axis=1), (1, h))
    log_g = jnp.log(1.0 - 2.0 ** (-5.0 - jnp.arange(h, dtype=F32)))
    idx = jnp.arange(CHUNK, dtype=F32)
    diff = idx[:, None] - idx[None, :]
    dmat = jnp.where(diff >= 0, jnp.exp(jnp.maximum(diff, 0.0)[None] * log_g[:, None, None]), 0.0)
    q_dec = jnp.exp((idx[None, :] + 1.0) * log_g[:, None])
    k_dec = jnp.exp((CHUNK - 1.0 - idx[None, :]) * log_g[:, None])
    c_dec = jnp.exp(CHUNK * log_g)
    expand = lambda z: jnp.repeat(z.T, d, axis=1)
    return (cos_t, sin_t, dmat, expand(q_dec), expand(k_dec), jnp.repeat(c_dec, d)[None, :])


def kernel(x, w_in, ln1_g, ln2_g, lnf_g, m_conv, m_b_i, m_b_f, m_ln_g, rw_mu, rw_w0, rw_w_up,
           rw_a0, rw_a_up, rw_g_up, rw_k_k, rw_k_a, rw_r_k, rw_ln_g, rw_ln_b, w_out, w_gate,
           w_up, w_down):
    bsz, t, d = x.shape
    depth = w_in.shape[0]
    n = bsz * t
    nc = t // CHUNK
    tm = min(512, n)
    tabs = _retention_tables(t)
    x2 = x.reshape(n, d)
    row = lambda z: z.reshape(1, -1)
    for l in range(depth):
        wl = w_in[l]
        w_pad = jnp.concatenate(
            [wl[:, :M_COLS], jnp.zeros((d, PM_W - M_COLS), wl.dtype), wl[:, M_COLS:]], axis=1).astype(BF16)
        pm, pr, pw = _in_proj(x2, row(ln1_g[l]), w_pad, tm)
        gates_t = pm[:, 4 * M_WIDTH:4 * M_WIDTH + 2 * M_HEADS].reshape(bsz, nc, CHUNK, 2 * M_HEADS)
        gates_t = gates_t.transpose(0, 1, 3, 2)
        lp = (m_conv[l], row(m_b_i[l]), row(m_b_f[l]), m_b_i[l].reshape(-1, 1), m_b_f[l].reshape(-1, 1),
              row(m_ln_g[l]), row(rw_mu[l]), row(rw_w0[l]), rw_w_up[l], row(rw_a0[l]), rw_a_up[l],
              rw_g_up[l], row(rw_k_k[l]), row(rw_k_a[l]), row(rw_r_k[l]), row(rw_ln_g[l]), row(rw_ln_b[l]))
        y = _mixers(pm.reshape(bsz, t, PM_W), pr.reshape(bsz, t, R_COLS), pw.reshape(bsz, t, W_COLS),
                    gates_t, tabs, lp)
        x2 = _out_proj(y.reshape(n, d), w_out[l].astype(BF16), x2, tm)
        x2 = _ffn(x2, row(ln2_g[l]), w_gate[l].astype(BF16), w_up[l].astype(BF16),
                  w_down[l].astype(BF16), row(lnf_g), tm, final_norm=(l == depth - 1))
    return x2.reshape(bsz, t, d)
```

```python
import functools
import math

import jax
import jax.numpy as jnp
from jax import lax
from jax.experimental import pallas as pl
from jax.experimental.pallas import tpu as pltpu

F32 = jnp.float32
BF16 = jnp.bfloat16

CHUNK = 64
HEAD_DIM = 64
M_HEADS, R_HEADS, W_HEADS = 4, 6, 6
M_WIDTH, R_WIDTH, W_WIDTH = 256, 384, 384
W_LORA, A_LORA, G_LORA = 64, 64, 128
M_COLS = 4 * M_WIDTH + 2 * M_HEADS
R_COLS = 4 * R_WIDTH
W_COLS = 3 * W_WIDTH + W_LORA + A_LORA + G_LORA
PM_W = 4 * M_WIDTH + 128
P_W = PM_W + R_COLS + W_COLS
RMS_EPS = 1e-6
HEAD_LN_EPS = 1e-5
RWKV_LN_EPS = 64e-5
ROPE_BASE = 10000.0
M_INIT = -1e30
VMEM_LIMIT = 56 * 1024 * 1024


def _bf(x):
    return x.astype(BF16)


def _dot(a, b):
    return jnp.dot(_bf(a), _bf(b), preferred_element_type=F32)


def _dot_nt(a, b):
    return lax.dot_general(_bf(a), _bf(b), (((1,), (1,)), ((), ())), preferred_element_type=F32)


def _dot_tn(a, b):
    return lax.dot_general(_bf(a), _bf(b), (((0,), (0,)), ((), ())), preferred_element_type=F32)


def _split2(x):
    hi = _bf(x)
    lo = _bf(x - hi.astype(F32))
    return hi, lo


def _dot3(a, b):
    ah, al = _split2(a)
    bh, bl = _split2(b)
    return (jnp.dot(ah, bh, preferred_element_type=F32)
            + jnp.dot(ah, bl, preferred_element_type=F32)
            + jnp.dot(al, bh, preferred_element_type=F32))


def _split3(x):
    x1 = _bf(x)
    r = x - x1.astype(F32)
    x2 = _bf(r)
    x3 = _bf(r - x2.astype(F32))
    return x1, x2, x3


def _dot_exact_rhs(m_bf, x):
    x1, x2, x3 = _split3(x)
    return (jnp.dot(m_bf, x1, preferred_element_type=F32)
            + jnp.dot(m_bf, x2, preferred_element_type=F32)
            + jnp.dot(m_bf, x3, preferred_element_type=F32))


def _dot_exact_lhs(x, m_bf):
    x1, x2, x3 = _split3(x)
    return (jnp.dot(x1, m_bf, preferred_element_type=F32)
            + jnp.dot(x2, m_bf, preferred_element_type=F32)
            + jnp.dot(x3, m_bf, preferred_element_type=F32))


def _sigmoid(x):
    return 1.0 / (1.0 + jnp.exp(-x))


def _silu(x):
    return x * _sigmoid(x)


def _softplus(x):
    return jnp.maximum(x, 0.0) + jnp.log(1.0 + jnp.exp(-jnp.abs(x)))


def _log_sigmoid(x):
    return -_softplus(-x)


def _rms(x, g):
    return x * lax.rsqrt(jnp.mean(x * x, axis=-1, keepdims=True) + RMS_EPS) * g


def _head_ln(x, eps):
    mu = jnp.mean(x, axis=-1, keepdims=True)
    xc = x - mu
    var = jnp.mean(xc * xc, axis=-1, keepdims=True)
    return xc * lax.rsqrt(var + eps)


def _shift_rows(cur, prev, j):
    row = lax.broadcasted_iota(jnp.int32, cur.shape, 0)
    return jnp.where(row >= j, pltpu.roll(cur, j, 0), pltpu.roll(prev, j, 0))


def _rot_half(x):
    pieces = []
    for i in range(x.shape[1] // 128):
        p = x[:, 128 * i:128 * (i + 1)]
        lane = lax.broadcasted_iota(jnp.int32, p.shape, 1)
        pieces.append(jnp.where((lane & 63) < 32, pltpu.roll(p, 96, 1), pltpu.roll(p, 32, 1)))
    return jnp.concatenate(pieces, axis=1)


def _in_kernel(x_ref, g_ref, w_ref, pm_ref, pr_ref, pw_ref):
    hb = _bf(_rms(x_ref[...], g_ref[...]))
    pm_ref[...] = jnp.dot(hb, w_ref[:, 0:PM_W], preferred_element_type=F32)
    pr_ref[...] = jnp.dot(hb, w_ref[:, PM_W:PM_W + R_COLS], preferred_element_type=F32)
    pw_ref[...] = jnp.dot(hb, w_ref[:, PM_W + R_COLS:P_W], preferred_element_type=F32)


def _in_proj(x2, g, w_pad, tm):
    n, d = x2.shape
    return pl.pallas_call(
        _in_kernel,
        grid=(n // tm,),
        in_specs=[pl.BlockSpec((tm, d), lambda i: (i, 0)),
                  pl.BlockSpec((1, d), lambda i: (0, 0)),
                  pl.BlockSpec((d, P_W), lambda i: (0, 0))],
        out_specs=[pl.BlockSpec((tm, PM_W), lambda i: (i, 0)),
                   pl.BlockSpec((tm, R_COLS), lambda i: (i, 0)),
                   pl.BlockSpec((tm, W_COLS), lambda i: (i, 0))],
        out_shape=[jax.ShapeDtypeStruct((n, PM_W), F32),
                   jax.ShapeDtypeStruct((n, R_COLS), F32),
                   jax.ShapeDtypeStruct((n, W_COLS), F32)],
        compiler_params=pltpu.CompilerParams(dimension_semantics=("arbitrary",),
                                             vmem_limit_bytes=VMEM_LIMIT),
        name="in_proj",
    )(x2, g, w_pad)


def _mlstm_heads(pm, gt, prev_qk, conv_w, b_i_row, b_f_row, b_i_col, b_f_col, ln_g,
                 c_scr, n_scr, m_scr, tril_bf, triu_bf, causal, outs):
    qk_raw = pm[:, 0:2 * M_WIDTH]
    acc = qk_raw * conv_w[3:4, :]
    for j in (1, 2, 3):
        acc = acc + _shift_rows(qk_raw, prev_qk, j) * conv_w[3 - j:4 - j, :]
    qk = _silu(acc)
    q = qk[:, 0:M_WIDTH]
    k = qk[:, M_WIDTH:2 * M_WIDTH] * (HEAD_DIM ** -0.5)
    v = pm[:, 2 * M_WIDTH:3 * M_WIDTH]
    o = pm[:, 3 * M_WIDTH:4 * M_WIDTH]
    gates = pm[:, 4 * M_WIDTH:4 * M_WIDTH + 2 * M_HEADS]
    i_col = gates[:, 0:M_HEADS] + b_i_row
    lf_col = _log_sigmoid(gates[:, M_HEADS:2 * M_HEADS] + b_f_row)
    i_row = gt[0:M_HEADS, :] + b_i_col
    lf_row = _log_sigmoid(gt[M_HEADS:2 * M_HEADS, :] + b_f_col)
    b_col = _dot_exact_rhs(tril_bf, lf_col)
    b_row = _dot_exact_lhs(lf_row, triu_bf)

    def head(h):
        sl = slice(HEAD_DIM * h, HEAD_DIM * (h + 1))
        qh, kh, vh = q[:, sl], k[:, sl], v[:, sl]
        bc = b_col[:, h:h + 1]
        ic = i_col[:, h:h + 1]
        m_old = m_scr[h]
        c_old = c_scr[h]
        n_old = n_scr[h]
        qk_t = _dot_nt(qh, kh)
        qc = _dot(qh, c_old)
        dlog = jnp.where(causal, bc - b_row[h:h + 1, :] + i_row[h:h + 1, :], -jnp.inf)
        inter = bc + m_old
        m_row = jnp.maximum(inter, jnp.max(dlog, axis=1, keepdims=True))
        b_last = bc[CHUNK - 1:CHUNK, :]
        src = b_last - bc + ic
        m_new = jnp.maximum(b_last + m_old, jnp.max(src, axis=0, keepdims=True))
        sw = jnp.exp(src - m_new)
        dec = jnp.exp(b_last + m_old - m_new)
        ksw = kh * sw
        c_scr[h] = dec * c_old + _dot_tn(ksw, vh)
        n_scr[h] = dec * n_old + jnp.sum(ksw, axis=0, keepdims=True)
        m_scr[h] = m_new
        yield
        s = qk_t * jnp.exp(dlog - m_row)
        inter_w = jnp.exp(inter - m_row)
        num = _dot(s, vh) + inter_w * qc
        den = jnp.sum(s, axis=1, keepdims=True) + inter_w * jnp.sum(qh * n_old, axis=1, keepdims=True)
        yield
        hh = num / jnp.maximum(jnp.abs(den), jnp.exp(-m_row))
        outs[h] = _sigmoid(o[:, sl]) * (_head_ln(hh, HEAD_LN_EPS) * ln_g[:, sl])

    return [head(h) for h in range(M_HEADS)], qk_raw


def _retention_heads(pr, cos, sin, dmat_ref, qdec, kdec, cdec, r_scr, outs):
    q = pr[:, 0:R_WIDTH]
    k = pr[:, R_WIDTH:2 * R_WIDTH]
    v = pr[:, 2 * R_WIDTH:3 * R_WIDTH]
    g = pr[:, 3 * R_WIDTH:4 * R_WIDTH]
    qr = q * cos + _rot_half(q) * sin
    kr = (k * cos + _rot_half(k) * sin) * (HEAD_DIM ** -0.5)
    qd = qr * qdec
    kd = kr * kdec

    def head(h):
        sl = slice(HEAD_DIM * h, HEAD_DIM * (h + 1))
        r_old = r_scr[h]
        qk_t = _dot_nt(qr[:, sl], kr[:, sl])
        qr_old = _dot(qd[:, sl], r_old)
        r_scr[h] = cdec[:, sl] * r_old + _dot_tn(kd[:, sl], v[:, sl])
        yield
        out = _dot(qk_t * dmat_ref[h], v[:, sl]) + qr_old
        yield
        outs[h] = _silu(g[:, sl]) * _head_ln(out, HEAD_LN_EPS)

    return [head(h) for h in range(R_HEADS)]


def _rwkv_heads(pw, prev_pw, mu, w0, w_up, a0, a_up, g_up, k_k, k_a, r_k, ln_g, ln_b,
                s_scr, tril_bf, lower_incl2, lower_strict2, eye, outs):
    ps = pw + mu * (_shift_rows(pw, prev_pw, 1) - pw)
    r = ps[:, 0:W_WIDTH]
    k = ps[:, W_WIDTH:2 * W_WIDTH]
    v = ps[:, 2 * W_WIDTH:3 * W_WIDTH]
    o1 = 3 * W_WIDTH
    wl = ps[:, o1:o1 + W_LORA]
    al = ps[:, o1 + W_LORA:o1 + W_LORA + A_LORA]
    gl = ps[:, o1 + W_LORA + A_LORA:o1 + W_LORA + A_LORA + G_LORA]
    w_log = -_softplus(-(w0 + _dot3(jnp.tanh(wl), w_up))) - 0.5
    lw = -jnp.exp(w_log)
    a = _sigmoid(a0 + _dot3(al, a_up))
    g = _dot(_sigmoid(gl), g_up)
    kk0 = k * k_k
    k2 = k * (1.0 + (a - 1.0) * k_a)
    cum = _dot_exact_rhs(tril_bf, lw)
    e_pos = jnp.exp(cum)
    e_neg = jnp.exp(-cum)
    e_prev = jnp.exp(cum - lw)
    e_last = jnp.exp(cum[CHUNK - 1:CHUNK, :] - cum)
    g_last = e_pos[CHUNK - 1:CHUNK, :]
    rk = r * k2 * r_k
    r_pos = r * e_pos
    k_neg = k2 * e_neg
    k_end = k2 * e_last

    def head(h):
        sl = slice(HEAD_DIM * h, HEAD_DIM * (h + 1))
        kk0h = kk0[:, sl]
        nrm = jnp.sqrt(jnp.sum(kk0h * kk0h, axis=1, keepdims=True))
        kkh = kk0h / jnp.maximum(nrm, 1e-12)
        vh = v[:, sl]
        b_t = -(kkh * a[:, sl])
        ar = jnp.concatenate([kkh * e_prev[:, sl], r_pos[:, sl]], axis=0)
        bk = jnp.concatenate([b_t * e_neg[:, sl], k_neg[:, sl]], axis=0)
        bk_end = jnp.concatenate([b_t * e_last[:, sl], k_end[:, sl]], axis=0)
        s_old = s_scr[h]
        m = _dot_nt(ar, bk)
        gs = _dot_nt(ar, s_old)
        yield
        top = jnp.where(lower_strict2, m[0:CHUNK, :], 0.0)
        bot = jnp.where(lower_incl2, m[CHUNK:2 * CHUNK, :], 0.0)
        p = top[:, 0:CHUNK]
        t_inv = eye + p
        rhs = gs[0:CHUNK, :] + _dot(top[:, CHUNK:2 * CHUNK], vh)
        p = _dot(p, p)
        yield
        for _ in range(4):
            t_inv = t_inv + _dot(t_inv, p)
            p = _dot(p, p)
            yield
        t_inv = t_inv + _dot(t_inv, p)
        yield
        u = _dot(t_inv, rhs)
        yield
        uv = jnp.concatenate([u, vh], axis=0)
        y = gs[CHUNK:2 * CHUNK, :] + _dot(bot, uv)
        s_scr[h] = s_old * g_last[:, sl] + _dot_tn(uv, bk_end)
        yield
        y = _head_ln(y, RWKV_LN_EPS) * ln_g[:, sl] + ln_b[:, sl]
        bonus = jnp.sum(rk[:, sl], axis=1, keepdims=True)
        outs[h] = (y + bonus * vh) * g[:, sl]

    return [head(h) for h in range(W_HEADS)]


def _run_interleaved(gens):
    gens = list(gens)
    while gens:
        for g in list(gens):
            try:
                next(g)
            except StopIteration:
                gens.remove(g)


def _mix_kernel(pm_ref, pr_ref, pw_ref, gt_ref, cos_ref, sin_ref, dmat_ref, qdec_ref, kdec_ref,
                cdec_ref, conv_ref, bi_row_ref, bf_row_ref, bi_col_ref, bf_col_ref, mln_ref,
                mu_ref, w0_ref, wup_ref, a0_ref, aup_ref, gup_ref, kk_ref, ka_ref, rk_ref,
                wlng_ref, wlnb_ref,
                y_ref,
                c_scr, n_scr, m_scr, r_scr, s_scr, prevqk_scr, prevpw_scr):
    @pl.when(pl.program_id(1) == 0)
    def _():
        c_scr[...] = jnp.zeros(c_scr.shape, F32)
        n_scr[...] = jnp.zeros(n_scr.shape, F32)
        m_scr[...] = jnp.full(m_scr.shape, M_INIT, F32)
        r_scr[...] = jnp.zeros(r_scr.shape, F32)
        s_scr[...] = jnp.zeros(s_scr.shape, F32)
        prevqk_scr[...] = jnp.zeros(prevqk_scr.shape, F32)
        prevpw_scr[...] = jnp.zeros(prevpw_scr.shape, F32)

    row = lax.broadcasted_iota(jnp.int32, (CHUNK, CHUNK), 0)
    col = lax.broadcasted_iota(jnp.int32, (CHUNK, CHUNK), 1)
    causal = col <= row
    tril_bf = jnp.where(causal, 1.0, 0.0).astype(BF16)
    triu_bf = jnp.where(row <= col, 1.0, 0.0).astype(BF16)
    eye = jnp.where(row == col, 1.0, 0.0).astype(F32)
    row2 = lax.broadcasted_iota(jnp.int32, (CHUNK, 2 * CHUNK), 0)
    col2 = lax.broadcasted_iota(jnp.int32, (CHUNK, 2 * CHUNK), 1) & (CHUNK - 1)
    lower_incl2 = col2 <= row2
    lower_strict2 = col2 < row2

    m_outs = [None] * M_HEADS
    r_outs = [None] * R_HEADS
    w_outs = [None] * W_HEADS
    pw = pw_ref[0]
    w_gens = _rwkv_heads(pw, prevpw_scr[...], mu_ref[...], w0_ref[...], wup_ref[...], a0_ref[...],
                         aup_ref[...], gup_ref[...], kk_ref[...], ka_ref[...], rk_ref[...],
                         wlng_ref[...], wlnb_ref[...], s_scr, tril_bf, lower_incl2, lower_strict2, eye,
                         w_outs)
    prevpw_scr[...] = pw
    m_gens, qk_raw = _mlstm_heads(
        pm_ref[0], gt_ref[0, 0], prevqk_scr[...], conv_ref[...], bi_row_ref[...], bf_row_ref[...],
        bi_col_ref[...], bf_col_ref[...], mln_ref[...], c_scr, n_scr, m_scr, tril_bf, triu_bf, causal,
        m_outs)
    prevqk_scr[...] = qk_raw
    r_gens = _retention_heads(pr_ref[0], cos_ref[...], sin_ref[...], dmat_ref, qdec_ref[...],
                              kdec_ref[...], cdec_ref[...], r_scr, r_outs)
    _run_interleaved(w_gens + m_gens + r_gens)

    y_ref[0] = jnp.concatenate(m_outs + r_outs + w_outs, axis=1).astype(y_ref.dtype)


def _full(shape):
    nd = len(shape)
    return pl.BlockSpec(shape, lambda b, c: (0,) * nd)


def _mixers(pm, pr, pw, gt, tabs, lp):
    bsz, t, _ = pm.shape
    nc = t // CHUNK
    cos, sin, dmat, qdec, kdec, cdec = tabs
    args = [pm, pr, pw, gt, cos, sin, dmat, qdec, kdec, cdec] + list(lp)
    in_specs = [
        pl.BlockSpec((1, CHUNK, PM_W), lambda b, c: (b, c, 0)),
        pl.BlockSpec((1, CHUNK, R_COLS), lambda b, c: (b, c, 0)),
        pl.BlockSpec((1, CHUNK, W_COLS), lambda b, c: (b, c, 0)),
        pl.BlockSpec((1, 1, 2 * M_HEADS, CHUNK), lambda b, c: (b, c, 0, 0)),
        pl.BlockSpec((CHUNK, R_WIDTH), lambda b, c: (c, 0)),
        pl.BlockSpec((CHUNK, R_WIDTH), lambda b, c: (c, 0)),
    ] + [_full(a.shape) for a in args[6:]]
    d_out = M_WIDTH + R_WIDTH + W_WIDTH
    return pl.pallas_call(
        _mix_kernel,
        grid=(bsz, nc),
        in_specs=in_specs,
        out_specs=pl.BlockSpec((1, CHUNK, d_out), lambda b, c: (b, c, 0)),
        out_shape=jax.ShapeDtypeStruct((bsz, t, d_out), BF16),
        scratch_shapes=[
            pltpu.VMEM((M_HEADS, HEAD_DIM, HEAD_DIM), F32),
            pltpu.VMEM((M_HEADS, 1, HEAD_DIM), F32),
            pltpu.VMEM((M_HEADS, 1, 1), F32),
            pltpu.VMEM((R_HEADS, HEAD_DIM, HEAD_DIM), F32),
            pltpu.VMEM((W_HEADS, HEAD_DIM, HEAD_DIM), F32),
            pltpu.VMEM((CHUNK, 2 * M_WIDTH), F32),
            pltpu.VMEM((CHUNK, W_COLS), F32),
        ],
        compiler_params=pltpu.CompilerParams(dimension_semantics=("arbitrary", "arbitrary"),
                                             vmem_limit_bytes=VMEM_LIMIT),
        name="mixers",
    )(*args)


def _out_kernel(y_ref, w_ref, x_ref, o_ref):
    o_ref[...] = x_ref[...] + jnp.dot(y_ref[...], w_ref[...], preferred_element_type=F32)


def _out_proj(y2, w_bf, x2, tm):
    n, d = x2.shape
    return pl.pallas_call(
        _out_kernel,
        grid=(n // tm,),
        in_specs=[pl.BlockSpec((tm, y2.shape[1]), lambda i: (i, 0)),
                  pl.BlockSpec(w_bf.shape, lambda i: (0, 0)),
                  pl.BlockSpec((tm, d), lambda i: (i, 0))],
        out_specs=pl.BlockSpec((tm, d), lambda i: (i, 0)),
        out_shape=jax.ShapeDtypeStruct((n, d), F32),
        compiler_params=pltpu.CompilerParams(dimension_semantics=("arbitrary",),
                                             vmem_limit_bytes=VMEM_LIMIT),
        name="out_proj",
    )(y2, w_bf, x2)


def _ffn_kernel(x_ref, g_ref, wg_ref, wu_ref, wd_ref, gf_ref, o_ref, *, ff_chunk, final_norm):
    x = x_ref[...]
    hb = _bf(_rms(x, g_ref[...]))
    acc = x
    d_ff = wg_ref.shape[1]
    for c in range(d_ff // ff_chunk):
        cs = slice(c * ff_chunk, (c + 1) * ff_chunk)
        gate = jnp.dot(hb, wg_ref[:, cs], preferred_element_type=F32)
        up = jnp.dot(hb, wu_ref[:, cs], preferred_element_type=F32)
        acc = acc + jnp.dot(_bf(_silu(gate) * up), wd_ref[cs, :], preferred_element_type=F32)
    if final_norm:
        acc = _rms(acc, gf_ref[...])
    o_ref[...] = acc


def _ffn(x2, g, wg, wu, wd, gfin, tm, final_norm):
    n, d = x2.shape
    d_ff = wg.shape[1]
    kern = functools.partial(_ffn_kernel, ff_chunk=256, final_norm=final_norm)
    return pl.pallas_call(
        kern,
        grid=(n // tm,),
        in_specs=[pl.BlockSpec((tm, d), lambda i: (i, 0)),
                  pl.BlockSpec((1, d), lambda i: (0, 0)),
                  pl.BlockSpec((d, d_ff), lambda i: (0, 0)),
                  pl.BlockSpec((d, d_ff), lambda i: (0, 0)),
                  pl.BlockSpec((d_ff, d), lambda i: (0, 0)),
                  pl.BlockSpec((1, d), lambda i: (0, 0))],
        out_specs=pl.BlockSpec((tm, d), lambda i: (i, 0)),
        out_shape=jax.ShapeDtypeStruct((n, d), F32),
        compiler_params=pltpu.CompilerParams(dimension_semantics=("arbitrary",),
                                             vmem_limit_bytes=VMEM_LIMIT),
        name="ffn",
    )(x2, g, wg, wu, wd, gfin)


def _retention_tables(t):
    h, d = R_HEADS, HEAD_DIM
    pos = jnp.arange(t, dtype=F32)
    theta = 1.0 / (ROPE_BASE ** jnp.linspace(0.0, 1.0, d // 2, dtype=F32))
    ang = pos[:, None] * theta[None, :]
    cos, sin = jnp.cos(ang), jnp.sin(ang)
    cos_t = jnp.tile(jnp.concatenate([cos, cos], axis=1), (1, h))
    sin_t = jnp.tile(jnp.concatenate([-sin, sin], axis=1), (1, h))
    log_g = jnp.log(1.0 - 2.0 ** (-5.0 - jnp.arange(h, dtype=F32)))
    idx = jnp.arange(CHUNK, dtype=F32)
    diff = idx[:, None] - idx[None, :]
    dmat = jnp.where(diff >= 0, jnp.exp(jnp.maximum(diff, 0.0)[None] * log_g[:, None, None]), 0.0)
    q_dec = jnp.exp((idx[None, :] + 1.0) * log_g[:, None])
    k_dec = jnp.exp((CHUNK - 1.0 - idx[None, :]) * log_g[:, None])
    c_dec = jnp.exp(CHUNK * log_g)
    expand = lambda z: jnp.repeat(z.T, d, axis=1)
    return (cos_t, sin_t, dmat, expand(q_dec), expand(k_dec), jnp.repeat(c_dec, d)[None, :])


def kernel(x, w_in, ln1_g, ln2_g, lnf_g, m_conv, m_b_i, m_b_f, m_ln_g, rw_mu, rw_w0, rw_w_up,
           rw_a0, rw_a_up, rw_g_up, rw_k_k, rw_k_a, rw_r_k, rw_ln_g, rw_ln_b, w_out, w_gate,
           w_up, w_down):
    bsz, t, d = x.shape
    depth = w_in.shape[0]
    n = bsz * t
    nc = t // CHUNK
    tm = min(512, n)
    tabs = _retention_tables(t)
    x2 = x.reshape(n, d)
    row = lambda z: z.reshape(1, -1)
    for l in range(depth):
        wl = w_in[l]
        w_pad = jnp.concatenate(
            [wl[:, :M_COLS], jnp.zeros((d, PM_W - M_COLS), wl.dtype), wl[:, M_COLS:]], axis=1).astype(BF16)
        pm, pr, pw = _in_proj(x2, row(ln1_g[l]), w_pad, tm)
        gates_t = pm[:, 4 * M_WIDTH:4 * M_WIDTH + 2 * M_HEADS].reshape(bsz, nc, CHUNK, 2 * M_HEADS)
        gates_t = gates_t.transpose(0, 1, 3, 2)
        lp = (m_conv[l], row(m_b_i[l]), row(m_b_f[l]), m_b_i[l].reshape(-1, 1), m_b_f[l].reshape(-1, 1),
              row(m_ln_g[l]), row(rw_mu[l]), row(rw_w0[l]), rw_w_up[l], row(rw_a0[l]), rw_a_up[l],
              rw_g_up[l], row(rw_k_k[l]), row(rw_k_a[l]), row(rw_r_k[l]), row(rw_ln_g[l]), row(rw_ln_b[l]))
        y = _mixers(pm.reshape(bsz, t, PM_W), pr.reshape(bsz, t, R_COLS), pw.reshape(bsz, t, W_COLS),
                    gates_t, tabs, lp)
        x2 = _out_proj(y.reshape(n, d), w_out[l].astype(BF16), x2, tm)
        x2 = _ffn(x2, row(ln2_g[l]), w_gate[l].astype(BF16), w_up[l].astype(BF16),
                  w_down[l].astype(BF16), row(lnf_g), tm, final_norm=(l == depth - 1))
    return x2.reshape(bsz, t, d)
```

```python
import functools
import math

import jax
import jax.numpy as jnp
from jax import lax
from jax.experimental import pallas as pl
from jax.experimental.pallas import tpu as pltpu

F32 = jnp.float32
BF16 = jnp.bfloat16

CHUNK = 64
HEAD_DIM = 64
M_HEADS, R_HEADS, W_HEADS = 4, 6, 6
M_WIDTH, R_WIDTH, W_WIDTH = 256, 384, 384
W_LORA, A_LORA, G_LORA = 64, 64, 128
M_COLS = 4 * M_WIDTH + 2 * M_HEADS
R_COLS = 4 * R_WIDTH
W_COLS = 3 * W_WIDTH + W_LORA + A_LORA + G_LORA
PM_W = 4 * M_WIDTH + 128
P_W = PM_W + R_COLS + W_COLS
RMS_EPS = 1e-6
HEAD_LN_EPS = 1e-5
RWKV_LN_EPS = 64e-5
ROPE_BASE = 10000.0
M_INIT = -1e30
VMEM_LIMIT = 56 * 1024 * 1024
TAIL = 8
NCH = 4
GROUP = 2
BD_W = 256


def _bf(x):
    return x.astype(BF16)


def _dot(a, b):
    return jnp.dot(_bf(a), _bf(b), preferred_element_type=F32)


def _dot_nt(a, b):
    return lax.dot_general(_bf(a), _bf(b), (((1,), (1,)), ((), ())), preferred_element_type=F32)


def _dot_tn(a, b):
    return lax.dot_general(_bf(a), _bf(b), (((0,), (0,)), ((), ())), preferred_element_type=F32)


def _split2(x):
    hi = _bf(x)
    lo = _bf(x - hi.astype(F32))
    return hi, lo


def _dot3(a, b):
    ah, al = _split2(a)
    bh, bl = _split2(b)
    return (jnp.dot(ah, bh, preferred_element_type=F32)
            + jnp.dot(ah, bl, preferred_element_type=F32)
            + jnp.dot(al, bh, preferred_element_type=F32))


def _split3(x):
    x1 = _bf(x)
    r = x - x1.astype(F32)
    x2 = _bf(r)
    x3 = _bf(r - x2.astype(F32))
    return x1, x2, x3


def _dot_exact_rhs(m_bf, x):
    x1, x2, x3 = _split3(x)
    return (jnp.dot(m_bf, x1, preferred_element_type=F32)
            + jnp.dot(m_bf, x2, preferred_element_type=F32)
            + jnp.dot(m_bf, x3, preferred_element_type=F32))


def _dot_exact_lhs(x, m_bf):
    x1, x2, x3 = _split3(x)
    return (jnp.dot(x1, m_bf, preferred_element_type=F32)
            + jnp.dot(x2, m_bf, preferred_element_type=F32)
            + jnp.dot(x3, m_bf, preferred_element_type=F32))


def _sigmoid(x):
    return 1.0 / (1.0 + jnp.exp(-x))


def _silu(x):
    return x * _sigmoid(x)


def _softplus(x):
    return jnp.maximum(x, 0.0) + jnp.log(1.0 + jnp.exp(-jnp.abs(x)))


def _log_sigmoid(x):
    return -_softplus(-x)


def _rms(x, g):
    return x * lax.rsqrt(jnp.mean(x * x, axis=-1, keepdims=True) + RMS_EPS) * g


def _rot_half(x):
    pieces = []
    for i in range(x.shape[1] // 128):
        p = x[:, 128 * i:128 * (i + 1)]
        lane = lax.broadcasted_iota(jnp.int32, p.shape, 1)
        pieces.append(jnp.where((lane & 63) < 32, pltpu.roll(p, 96, 1), pltpu.roll(p, 32, 1)))
    return jnp.concatenate(pieces, axis=1)


def _in_kernel(x_ref, g_ref, w_ref, pm_ref, pr_ref, pw_ref):
    hb = _bf(_rms(x_ref[...], g_ref[...]))
    pm_ref[...] = jnp.dot(hb, w_ref[:, 0:PM_W], preferred_element_type=F32)
    pr_ref[...] = jnp.dot(hb, w_ref[:, PM_W:PM_W + R_COLS], preferred_element_type=F32)
    pw_ref[...] = jnp.dot(hb, w_ref[:, PM_W + R_COLS:P_W], preferred_element_type=F32)


def _in_proj(x2, g, w_pad, tm):
    n, d = x2.shape
    return pl.pallas_call(
        _in_kernel,
        grid=(n // tm,),
        in_specs=[pl.BlockSpec((tm, d), lambda i: (i, 0)),
                  pl.BlockSpec((1, d), lambda i: (0, 0)),
                  pl.BlockSpec((d, P_W), lambda i: (0, 0))],
        out_specs=[pl.BlockSpec((tm, PM_W), lambda i: (i, 0)),
                   pl.BlockSpec((tm, R_COLS), lambda i: (i, 0)),
                   pl.BlockSpec((tm, W_COLS), lambda i: (i, 0))],
        out_shape=[jax.ShapeDtypeStruct((n, PM_W), F32),
                   jax.ShapeDtypeStruct((n, R_COLS), F32),
                   jax.ShapeDtypeStruct((n, W_COLS), F32)],
        compiler_params=pltpu.CompilerParams(dimension_semantics=("arbitrary",),
                                             vmem_limit_bytes=VMEM_LIMIT),
        name="in_proj",
    )(x2, g, w_pad)


def _rows(c):
    return slice(c * CHUNK, (c + 1) * CHUNK)


def _grows(g):
    return slice(g * GROUP * CHUNK, (g + 1) * GROUP * CHUNK)


def _lanes(h):
    return slice(h * HEAD_DIM, (h + 1) * HEAD_DIM)


def _shift_rows(cur, prev_tail, j):
    rolled = pltpu.roll(cur, j, 0)
    row = lax.broadcasted_iota(jnp.int32, prev_tail.shape, 0)
    head = jnp.where(row >= j, rolled[0:TAIL, :], pltpu.roll(prev_tail, j, 0))
    return jnp.concatenate([head, rolled[TAIL:, :]], axis=0)


def _cummax_rows(x):
    row = lax.broadcasted_iota(jnp.int32, x.shape, 0)
    step = 1
    while step < x.shape[0]:
        x = jnp.maximum(x, jnp.where(row >= step, pltpu.roll(x, step, 0), -jnp.inf))
        step *= 2
    return x


def _head_sums(x, bd_bf):
    outs = []
    off = 0
    while off < x.shape[1]:
        n = min(BD_W, x.shape[1] - off)
        hi, lo = _split2(x[:, off:off + n])
        b = bd_bf[0:n, 0:n]
        outs.append(jnp.dot(hi, b, preferred_element_type=F32) + jnp.dot(lo, b, preferred_element_type=F32))
        off += n
    return jnp.concatenate(outs, axis=1)


def _heads_ln(x, eps, bd_bf):
    mu = _head_sums(x, bd_bf) * (1.0 / HEAD_DIM)
    xc = x - mu
    var = _head_sums(xc * xc, bd_bf) * (1.0 / HEAD_DIM)
    return xc * lax.rsqrt(var + eps)


def _mlstm_heads(pm, gt, prev_tail, conv_w, gate_bias, b_i_col, b_f_col, cn_scr, m_scr, pre_ref,
                 tril_bf, triu_bf, causal_g, nch):
    chunks = range(nch)
    groups = range(nch // GROUP)
    qk_raw = pm[:, 0:2 * M_WIDTH]
    acc = qk_raw * conv_w[3:4, :]
    for j in (1, 2, 3):
        acc = acc + _shift_rows(qk_raw, prev_tail, j) * conv_w[3 - j:4 - j, :]
    qk = _silu(acc)
    q = qk[:, 0:M_WIDTH]
    k = qk[:, M_WIDTH:2 * M_WIDTH] * (HEAD_DIM ** -0.5)
    v = pm[:, 2 * M_WIDTH:3 * M_WIDTH]
    g128 = pm[:, 4 * M_WIDTH:4 * M_WIDTH + 128] + gate_bias
    lf128 = _log_sigmoid(g128)
    m = m_scr[...]
    b_al, cm, m_old, m_new, b_last = [], [], [], [], []
    for c in chunks:
        b_c = pltpu.roll(_dot_exact_rhs(tril_bf, lf128[_rows(c), :]), 128 - M_HEADS, 1)
        cm_c = _cummax_rows(g128[_rows(c), :] - b_c)
        bl = b_c[CHUNK - 1:CHUNK, :]
        b_al.append(b_c)
        cm.append(cm_c)
        b_last.append(bl)
        m_old.append(m)
        m = bl + jnp.maximum(m, cm_c[CHUNK - 1:CHUNK, :])
        m_new.append(m)
    m_scr[...] = m
    cat = lambda xs: jnp.concatenate(xs, axis=0)
    top = [jnp.maximum(m_old[c], cm[c]) for c in chunks]
    neg_top = cat([-t for t in top])
    inter_w = cat([jnp.exp(m_old[c] - top[c]) for c in chunks])
    e_mrow = cat([jnp.exp(-(b_al[c] + top[c])) for c in chunks])
    sw = cat([jnp.exp(b_last[c] - b_al[c] + g128[_rows(c), :] - m_new[c]) for c in chunks])
    dec = [jnp.exp(b_last[c] + m_old[c] - m_new[c]) for c in chunks]
    d_row = []
    for g in groups:
        parts = []
        for c in range(g * GROUP, (g + 1) * GROUP):
            i_r = gt[c, 0:M_HEADS, :] + b_i_col
            lf_r = _log_sigmoid(gt[c, M_HEADS:2 * M_HEADS, :] + b_f_col)
            parts.append(i_r - _dot_exact_lhs(lf_r, triu_bf))
        d_row.append(jnp.concatenate(parts, axis=1))
    ones = jnp.ones((GROUP * CHUNK, HEAD_DIM), F32)

    def head(h):
        sl = _lanes(h)
        hc = slice(h, h + 1)
        qk_t = [_dot_nt(q[_grows(g), sl], k[_grows(g), sl]) for g in groups]
        vaug = [jnp.concatenate([v[_grows(g), sl], ones], axis=1) for g in groups]
        cn_in = [_dot_tn(k[_rows(c), sl] * sw[_rows(c), hc],
                         vaug[c // GROUP][_rows(c % GROUP), :]) for c in chunks]
        yield
        e = [jnp.exp(jnp.where(causal_g, neg_top[_grows(g), hc] + d_row[g][hc, :], -jnp.inf))
             for g in groups]
        sva = [_dot(qk_t[g] * e[g], vaug[g]) for g in groups]
        cn = cn_scr[h]
        qcn = []
        for c in chunks:
            qcn.append(_dot(q[_rows(c), sl], cn))
            cn = dec[c][:, hc] * cn + cn_in[c]
        cn_scr[h] = cn
        yield
        for g in groups:
            nd = sva[g] + inter_w[_grows(g), hc] * jnp.concatenate(qcn[g * GROUP:(g + 1) * GROUP], axis=0)
            pre_ref[_grows(g), sl] = nd[:, 0:HEAD_DIM] / jnp.maximum(
                jnp.abs(nd[:, HEAD_DIM:2 * HEAD_DIM]), e_mrow[_grows(g), hc])

    return [head(h) for h in range(M_HEADS)], qk_raw


def _mlstm_finish(pm, pre, ln_g, bd_bf):
    o = pm[:, 3 * M_WIDTH:4 * M_WIDTH]
    return _sigmoid(o) * (_heads_ln(pre, HEAD_LN_EPS, bd_bf) * ln_g)


def _retention_heads(pr, cos, sin, dmat_ref, qdec, kdec, cdec, r_scr, pre_ref, nch):
    chunks = range(nch)
    groups = range(nch // GROUP)
    q = pr[:, 0:R_WIDTH]
    k = pr[:, R_WIDTH:2 * R_WIDTH]
    v = pr[:, 2 * R_WIDTH:3 * R_WIDTH]
    qr = q * cos + _rot_half(q) * sin
    kr = (k * cos + _rot_half(k) * sin) * (HEAD_DIM ** -0.5)
    qd = qr * qdec
    kd = kr * kdec

    def head(h):
        sl = _lanes(h)
        qk_t = [_dot_nt(qr[_grows(g), sl], kr[_grows(g), sl]) for g in groups]
        r_in = [_dot_tn(kd[_rows(c), sl], v[_rows(c), sl]) for c in chunks]
        yield
        intra = [_dot(qk_t[g] * dmat_ref[h], v[_grows(g), sl]) for g in groups]
        r_st = r_scr[h]
        cross = []
        for c in chunks:
            cross.append(_dot(qd[_rows(c), sl], r_st))
            r_st = cdec[:, sl] * r_st + r_in[c]
        r_scr[h] = r_st
        yield
        for g in groups:
            pre_ref[_grows(g), sl] = intra[g] + jnp.concatenate(cross[g * GROUP:(g + 1) * GROUP], axis=0)

    return [head(h) for h in range(R_HEADS)]


def _retention_finish(pr, pre, bd_bf):
    return _silu(pr[:, 3 * R_WIDTH:4 * R_WIDTH]) * _heads_ln(pre, HEAD_LN_EPS, bd_bf)


def _rwkv_heads(pw, prev_tail, mu, w0, w_up, a0, a_up, g_up, k_k, k_a, r_k,
                s_scr, pre_ref, tril_bf, bd_bf, lower_incl2, lower_strict2, nch):
    chunks = range(nch)
    groups = range(nch // GROUP)
    rg_n = GROUP * CHUNK
    ps = pw + mu * (_shift_rows(pw, prev_tail, 1) - pw)
    r = ps[:, 0:W_WIDTH]
    k = ps[:, W_WIDTH:2 * W_WIDTH]
    v = ps[:, 2 * W_WIDTH:3 * W_WIDTH]
    o1 = 3 * W_WIDTH
    wl = ps[:, o1:o1 + W_LORA]
    al = ps[:, o1 + W_LORA:o1 + W_LORA + A_LORA]
    gl = ps[:, o1 + W_LORA + A_LORA:o1 + W_LORA + A_LORA + G_LORA]
    w_log = -_softplus(-(w0 + _dot3(jnp.tanh(wl), w_up))) - 0.5
    lw = -jnp.exp(w_log)
    a = _sigmoid(a0 + _dot3(al, a_up))
    g = _dot(_sigmoid(gl), g_up)
    kk0 = k * k_k
    kk = kk0 / jnp.maximum(jnp.sqrt(_head_sums(kk0 * kk0, bd_bf)), 1e-12)
    k2 = k * (1.0 + (a - 1.0) * k_a)
    cum_c = [_dot_exact_rhs(tril_bf, lw[_rows(c), :]) for c in chunks]
    cum = jnp.concatenate(cum_c, axis=0)
    to_end = jnp.concatenate([cum_c[c][CHUNK - 1:CHUNK, :] - cum_c[c] for c in chunks], axis=0)
    e_pos = jnp.exp(cum)
    e_neg = jnp.exp(-cum)
    e_last = jnp.exp(to_end)
    b_t = -(kk * a)
    a_prev = kk * jnp.exp(cum - lw)
    b_neg = b_t * e_neg
    b_end = b_t * e_last
    r_pos = r * e_pos
    k_neg = k2 * e_neg
    k_end = k2 * e_last
    bonus_v = _head_sums(r * k2 * r_k, bd_bf) * v

    def head(h):
        sl = _lanes(h)
        ar = [jnp.concatenate([a_prev[_grows(q_), sl], r_pos[_grows(q_), sl]], axis=0) for q_ in groups]
        bk = [jnp.concatenate([b_neg[_grows(q_), sl], k_neg[_grows(q_), sl]], axis=0) for q_ in groups]
        vg = [v[_grows(q_), sl] for q_ in groups]
        m = [_dot_nt(ar[q_], bk[q_]) for q_ in groups]
        yield
        top = [jnp.where(lower_strict2, m[q_][0:rg_n, :], 0.0) for q_ in groups]
        bot = [jnp.where(lower_incl2, m[q_][rg_n:2 * rg_n, :], 0.0) for q_ in groups]
        p = [top[q_][:, 0:rg_n] for q_ in groups]
        akv = [_dot(top[q_][:, rg_n:2 * rg_n], vg[q_]) for q_ in groups]
        yield
        x = [jnp.concatenate([ar[q_][0:rg_n, :], akv[q_]], axis=1) for q_ in groups]
        for _ in range(5):
            px = [_dot(p[q_], jnp.concatenate([x[q_], p[q_]], axis=1)) for q_ in groups]
            x = [x[q_] + px[q_][:, 0:2 * HEAD_DIM] for q_ in groups]
            p = [px[q_][:, 2 * HEAD_DIM:2 * HEAD_DIM + rg_n] for q_ in groups]
            yield
        x = [x[q_] + _dot(p[q_], x[q_]) for q_ in groups]
        yield
        s_st = s_scr[h]
        for q_ in groups:
            us, rss = [], []
            for j in range(GROUP):
                c = q_ * GROUP + j
                rs = _rows(c)
                xj = x[q_][_rows(j), :]
                wr = jnp.concatenate([xj[:, 0:HEAD_DIM], r_pos[rs, sl]], axis=0)
                gs = _dot_nt(wr, s_st)
                yield
                u = gs[0:CHUNK, :] + xj[:, HEAD_DIM:2 * HEAD_DIM]
                uv = jnp.concatenate([u, v[rs, sl]], axis=0)
                bk_end = jnp.concatenate([b_end[rs, sl], k_end[rs, sl]], axis=0)
                s_st = s_st * e_pos[(c + 1) * CHUNK - 1:(c + 1) * CHUNK, sl] + _dot_tn(uv, bk_end)
                us.append(u)
                rss.append(gs[CHUNK:2 * CHUNK, :])
                yield
            uvg = jnp.concatenate(us + [vg[q_]], axis=0)
            pre_ref[_grows(q_), sl] = jnp.concatenate(rss, axis=0) + _dot(bot[q_], uvg)
        s_scr[h] = s_st

    return [head(h) for h in range(W_HEADS)], (g, bonus_v)


def _rwkv_finish(pre, g, bonus_v, ln_g, ln_b, bd_bf):
    return (_heads_ln(pre, RWKV_LN_EPS, bd_bf) * ln_g + ln_b + bonus_v) * g


def _run_interleaved(gens):
    pending = list(gens)
    while pending:
        for g in list(pending):
            try:
                next(g)
            except StopIteration:
                pending.remove(g)


def _mix_kernel(pm_ref, pr_ref, pw_ref, gt_ref, cos_ref, sin_ref, dmat_ref, qdec_ref, kdec_ref,
                cdec_ref, conv_ref, gbias_ref, bi_col_ref, bf_col_ref, mln_ref,
                mu_ref, w0_ref, wup_ref, a0_ref, aup_ref, gup_ref, kk_ref, ka_ref, rk_ref,
                wlng_ref, wlnb_ref,
                y_ref,
                cn_scr, m_scr, r_scr, s_scr, prevqk_scr, prevpw_scr, prem_scr, prer_scr, prew_scr,
                *, nch):
    @pl.when(pl.program_id(1) == 0)
    def _():
        cn_scr[...] = jnp.zeros(cn_scr.shape, F32)
        m_scr[...] = jnp.full(m_scr.shape, M_INIT, F32)
        r_scr[...] = jnp.zeros(r_scr.shape, F32)
        s_scr[...] = jnp.zeros(s_scr.shape, F32)
        prevqk_scr[...] = jnp.zeros(prevqk_scr.shape, F32)
        prevpw_scr[...] = jnp.zeros(prevpw_scr.shape, F32)

    rg_n = GROUP * CHUNK
    row = lax.broadcasted_iota(jnp.int32, (CHUNK, CHUNK), 0)
    col = lax.broadcasted_iota(jnp.int32, (CHUNK, CHUNK), 1)
    tril_bf = jnp.where(col <= row, 1.0, 0.0).astype(BF16)
    triu_bf = jnp.where(row <= col, 1.0, 0.0).astype(BF16)
    rowb = lax.broadcasted_iota(jnp.int32, (BD_W, BD_W), 0)
    colb = lax.broadcasted_iota(jnp.int32, (BD_W, BD_W), 1)
    bd_bf = jnp.where((rowb >> 6) == (colb >> 6), 1.0, 0.0).astype(BF16)
    row2 = lax.broadcasted_iota(jnp.int32, (rg_n, 2 * rg_n), 0)
    col2 = lax.broadcasted_iota(jnp.int32, (rg_n, 2 * rg_n), 1) & (rg_n - 1)
    start2 = (row2 >> 6) << 6
    lower_incl2 = (col2 <= row2) & (col2 >= start2)
    lower_strict2 = (col2 < row2) & (col2 >= start2)
    causal_g = lower_incl2[:, 0:rg_n]

    tt = nch * CHUNK
    pw = pw_ref[0]
    w_gens, (w_gate, bonus_v) = _rwkv_heads(
        pw, prevpw_scr[...], mu_ref[...], w0_ref[...], wup_ref[...], a0_ref[...], aup_ref[...],
        gup_ref[...], kk_ref[...], ka_ref[...], rk_ref[...], s_scr, prew_scr, tril_bf, bd_bf,
        lower_incl2, lower_strict2, nch)
    prevpw_scr[...] = pw[tt - TAIL:tt, :]
    pm = pm_ref[0]
    m_gens, qk_raw = _mlstm_heads(pm, gt_ref[0], prevqk_scr[...], conv_ref[...], gbias_ref[...],
                                  bi_col_ref[...], bf_col_ref[...], cn_scr, m_scr, prem_scr,
                                  tril_bf, triu_bf, causal_g, nch)
    prevqk_scr[...] = qk_raw[tt - TAIL:tt, :]
    pr = pr_ref[0]
    r_gens = _retention_heads(pr, cos_ref[...], sin_ref[...], dmat_ref, qdec_ref[...],
                              kdec_ref[...], cdec_ref[...], r_scr, prer_scr, nch)
    _run_interleaved(w_gens + m_gens + r_gens)

    y_ref[0, :, 0:M_WIDTH] = _mlstm_finish(pm, prem_scr[...], mln_ref[...], bd_bf).astype(y_ref.dtype)
    y_ref[0, :, M_WIDTH:M_WIDTH + R_WIDTH] = _retention_finish(pr, prer_scr[...], bd_bf).astype(y_ref.dtype)
    y_ref[0, :, M_WIDTH + R_WIDTH:M_WIDTH + R_WIDTH + W_WIDTH] = _rwkv_finish(
        prew_scr[...], w_gate, bonus_v, wlng_ref[...], wlnb_ref[...], bd_bf).astype(y_ref.dtype)


def _full(shape):
    nd = len(shape)
    return pl.BlockSpec(shape, lambda b, c: (0,) * nd)


def _mixers(pm, pr, pw, gt, tabs, lp, nch):
    bsz, t, _ = pm.shape
    tt = nch * CHUNK
    cos, sin, dmat, qdec, kdec, cdec = tabs
    qdec, kdec = jnp.tile(qdec, (nch, 1)), jnp.tile(kdec, (nch, 1))
    eye_g = jnp.eye(GROUP, dtype=dmat.dtype)
    dmat_g = jnp.einsum('ab,hij->haibj', eye_g, dmat).reshape(
        dmat.shape[0], GROUP * CHUNK, GROUP * CHUNK)
    args = [pm, pr, pw, gt, cos, sin, dmat_g, qdec, kdec, cdec] + list(lp)
    in_specs = [
        pl.BlockSpec((1, tt, PM_W), lambda b, c: (b, c, 0)),
        pl.BlockSpec((1, tt, R_COLS), lambda b, c: (b, c, 0)),
        pl.BlockSpec((1, tt, W_COLS), lambda b, c: (b, c, 0)),
        pl.BlockSpec((1, nch, 2 * M_HEADS, CHUNK), lambda b, c: (b, c, 0, 0)),
        pl.BlockSpec((tt, R_WIDTH), lambda b, c: (c, 0)),
        pl.BlockSpec((tt, R_WIDTH), lambda b, c: (c, 0)),
    ] + [_full(a.shape) for a in args[6:]]
    d_out = M_WIDTH + R_WIDTH + W_WIDTH
    return pl.pallas_call(
        functools.partial(_mix_kernel, nch=nch),
        grid=(bsz, t // tt),
        in_specs=in_specs,
        out_specs=pl.BlockSpec((1, tt, d_out), lambda b, c: (b, c, 0)),
        out_shape=jax.ShapeDtypeStruct((bsz, t, d_out), BF16),
        scratch_shapes=[
            pltpu.VMEM((M_HEADS, HEAD_DIM, 2 * HEAD_DIM), F32),
            pltpu.VMEM((1, 128), F32),
            pltpu.VMEM((R_HEADS, HEAD_DIM, HEAD_DIM), F32),
            pltpu.VMEM((W_HEADS, HEAD_DIM, HEAD_DIM), F32),
            pltpu.VMEM((TAIL, 2 * M_WIDTH), F32),
            pltpu.VMEM((TAIL, W_COLS), F32),
            pltpu.VMEM((tt, M_WIDTH), F32),
            pltpu.VMEM((tt, R_WIDTH), F32),
            pltpu.VMEM((tt, W_WIDTH), F32),
        ],
        compiler_params=pltpu.CompilerParams(dimension_semantics=("arbitrary", "arbitrary"),
                                             vmem_limit_bytes=VMEM_LIMIT),
        name="mixers",
    )(*args)


def _out_kernel(y_ref, w_ref, x_ref, o_ref):
    o_ref[...] = x_ref[...] + jnp.dot(y_ref[...], w_ref[...], preferred_element_type=F32)


def _out_proj(y2, w_bf, x2, tm):
    n, d = x2.shape
    return pl.pallas_call(
        _out_kernel,
        grid=(n // tm,),
        in_specs=[pl.BlockSpec((tm, y2.shape[1]), lambda i: (i, 0)),
                  pl.BlockSpec(w_bf.shape, lambda i: (0, 0)),
                  pl.BlockSpec((tm, d), lambda i: (i, 0))],
        out_specs=pl.BlockSpec((tm, d), lambda i: (i, 0)),
        out_shape=jax.ShapeDtypeStruct((n, d), F32),
        compiler_params=pltpu.CompilerParams(dimension_semantics=("arbitrary",),
                                             vmem_limit_bytes=VMEM_LIMIT),
        name="out_proj",
    )(y2, w_bf, x2)


def _ffn_kernel(x_ref, g_ref, wg_ref, wu_ref, wd_ref, gf_ref, o_ref, *, ff_chunk, final_norm):
    x = x_ref[...]
    hb = _bf(_rms(x, g_ref[...]))
    acc = x
    d_ff = wg_ref.shape[1]
    for c in range(d_ff // ff_chunk):
        cs = slice(c * ff_chunk, (c + 1) * ff_chunk)
        gate = jnp.dot(hb, wg_ref[:, cs], preferred_element_type=F32)
        up = jnp.dot(hb, wu_ref[:, cs], preferred_element_type=F32)
        acc = acc + jnp.dot(_bf(_silu(gate) * up), wd_ref[cs, :], preferred_element_type=F32)
    if final_norm:
        acc = _rms(acc, gf_ref[...])
    o_ref[...] = acc


def _ffn(x2, g, wg, wu, wd, gfin, tm, final_norm):
    n, d = x2.shape
    d_ff = wg.shape[1]
    kern = functools.partial(_ffn_kernel, ff_chunk=256, final_norm=final_norm)
    return pl.pallas_call(
        kern,
        grid=(n // tm,),
        in_specs=[pl.BlockSpec((tm, d), lambda i: (i, 0)),
                  pl.BlockSpec((1, d), lambda i: (0, 0)),
                  pl.BlockSpec((d, d_ff), lambda i: (0, 0)),
                  pl.BlockSpec((d, d_ff), lambda i: (0, 0)),
                  pl.BlockSpec((d_ff, d), lambda i: (0, 0)),
                  pl.BlockSpec((1, d), lambda i: (0, 0))],
        out_specs=pl.BlockSpec((tm, d), lambda i: (i, 0)),
        out_shape=jax.ShapeDtypeStruct((n, d), F32),
        compiler_params=pltpu.CompilerParams(dimension_semantics=("arbitrary",),
                                             vmem_limit_bytes=VMEM_LIMIT),
        name="ffn",
    )(x2, g, wg, wu, wd, gfin)


def _retention_tables(t):
    h, d = R_HEADS, HEAD_DIM
    pos = jnp.arange(t, dtype=F32)
    theta = 1.0 / (ROPE_BASE ** jnp.linspace(0.0, 1.0, d // 2, dtype=F32))
    ang = pos[:, None] * theta[None, :]
    cos, sin = jnp.cos(ang), jnp.sin(ang)
    cos_t = jnp.tile(jnp.concatenate([cos, cos], axis=1), (1, h))
    sin_t = jnp.tile(jnp.concatenate([-sin, sin], axis=1), (1, h))
    log_g = jnp.log(1.0 - 2.0 ** (-5.0 - jnp.arange(h, dtype=F32)))
    idx = jnp.arange(CHUNK, dtype=F32)
    diff = idx[:, None] - idx[None, :]
    dmat = jnp.where(diff >= 0, jnp.exp(jnp.maximum(diff, 0.0)[None] * log_g[:, None, None]), 0.0)
    q_dec = jnp.exp((idx[None, :] + 1.0) * log_g[:, None])
    k_dec = jnp.exp((CHUNK - 1.0 - idx[None, :]) * log_g[:, None])
    c_dec = jnp.exp(CHUNK * log_g)
    expand = lambda z: jnp.repeat(z.T, d, axis=1)
    return (cos_t, sin_t, dmat, expand(q_dec), expand(k_dec), jnp.repeat(c_dec, d)[None, :])


def kernel(x, w_in, ln1_g, ln2_g, lnf_g, m_conv, m_b_i, m_b_f, m_ln_g, rw_mu, rw_w0, rw_w_up,
           rw_a0, rw_a_up, rw_g_up, rw_k_k, rw_k_a, rw_r_k, rw_ln_g, rw_ln_b, w_out, w_gate,
           w_up, w_down):
    bsz, t, d = x.shape
    depth = w_in.shape[0]
    n = bsz * t
    nc = t // CHUNK
    tm = min(512, n)
    nch = math.gcd(nc, NCH)
    tabs = _retention_tables(t)
    x2 = x.reshape(n, d)
    row = lambda z: z.reshape(1, -1)
    for l in range(depth):
        wl = w_in[l]
        w_pad = jnp.concatenate(
            [wl[:, :M_COLS], jnp.zeros((d, PM_W - M_COLS), wl.dtype), wl[:, M_COLS:]], axis=1).astype(BF16)
        pm, pr, pw = _in_proj(x2, row(ln1_g[l]), w_pad, tm)
        gates_t = pm[:, 4 * M_WIDTH:4 * M_WIDTH + 2 * M_HEADS].reshape(bsz, nc, CHUNK, 2 * M_HEADS)
        gates_t = gates_t.transpose(0, 1, 3, 2)
        gate_bias = jnp.concatenate([m_b_i[l], m_b_f[l], jnp.zeros((128 - 2 * M_HEADS,), F32)])[None, :]
        lp = (m_conv[l], gate_bias, m_b_i[l].reshape(-1, 1), m_b_f[l].reshape(-1, 1),
              row(m_ln_g[l]), row(rw_mu[l]), row(rw_w0[l]), rw_w_up[l], row(rw_a0[l]), rw_a_up[l],
              rw_g_up[l], row(rw_k_k[l]), row(rw_k_a[l]), row(rw_r_k[l]), row(rw_ln_g[l]), row(rw_ln_b[l]))
        y = _mixers(pm.reshape(bsz, t, PM_W), pr.reshape(bsz, t, R_COLS), pw.reshape(bsz, t, W_COLS),
                    gates_t, tabs, lp, nch)
        x2 = _out_proj(y.reshape(n, d), w_out[l].astype(BF16), x2, tm)
        x2 = _ffn(x2, row(ln2_g[l]), w_gate[l].astype(BF16), w_up[l].astype(BF16),
                  w_down[l].astype(BF16), row(lnf_g), tm, final_norm=(l == depth - 1))
    return x2.reshape(bsz, t, d)
```

```python
import functools
import math

import jax
import jax.numpy as jnp
from jax import lax
from jax.experimental import pallas as pl
from jax.experimental.pallas import tpu as pltpu

F32 = jnp.float32
BF16 = jnp.bfloat16

CHUNK = 64
HEAD_DIM = 64
M_HEADS, R_HEADS, W_HEADS = 4, 6, 6
M_WIDTH, R_WIDTH, W_WIDTH = 256, 384, 384
W_LORA, A_LORA, G_LORA = 64, 64, 128
M_COLS = 4 * M_WIDTH + 2 * M_HEADS
R_COLS = 4 * R_WIDTH
W_COLS = 3 * W_WIDTH + W_LORA + A_LORA + G_LORA
PM_W = 4 * M_WIDTH + 128
P_W = PM_W + R_COLS + W_COLS
RMS_EPS = 1e-6
HEAD_LN_EPS = 1e-5
RWKV_LN_EPS = 64e-5
ROPE_BASE = 10000.0
M_INIT = -1e30
VMEM_LIMIT = 56 * 1024 * 1024
TAIL = 8
NCH = 4
GROUP = 2
BD_W = 256
FF_CHUNK = 256
GT_ROWS = 16


def _bf(x):
    return x.astype(BF16)


def _dot(a, b):
    return jnp.dot(_bf(a), _bf(b), preferred_element_type=F32)


def _dot_nt(a, b):
    return lax.dot_general(_bf(a), _bf(b), (((1,), (1,)), ((), ())), preferred_element_type=F32)


def _dot_tn(a, b):
    return lax.dot_general(_bf(a), _bf(b), (((0,), (0,)), ((), ())), preferred_element_type=F32)


def _split2(x):
    hi = _bf(x)
    lo = _bf(x - hi.astype(F32))
    return hi, lo


def _split3(x):
    x1 = _bf(x)
    r = x - x1.astype(F32)
    x2 = _bf(r)
    x3 = _bf(r - x2.astype(F32))
    return x1, x2, x3


def _dot_exact_rhs(m_bf, x):
    x1, x2, x3 = _split3(x)
    return (jnp.dot(m_bf, x1, preferred_element_type=F32)
            + jnp.dot(m_bf, x2, preferred_element_type=F32)
            + jnp.dot(m_bf, x3, preferred_element_type=F32))


def _dot_exact_lhs(x, m_bf):
    x1, x2, x3 = _split3(x)
    return (jnp.dot(x1, m_bf, preferred_element_type=F32)
            + jnp.dot(x2, m_bf, preferred_element_type=F32)
            + jnp.dot(x3, m_bf, preferred_element_type=F32))


def _sigmoid(x):
    return 1.0 / (1.0 + jnp.exp(-x))


def _silu(x):
    return x * _sigmoid(x)


def _softplus(x):
    return jnp.maximum(x, 0.0) + jnp.log(1.0 + jnp.exp(-jnp.abs(x)))


def _log_sigmoid(x):
    return -_softplus(-x)


def _rms(x, g):
    return x * lax.rsqrt(jnp.mean(x * x, axis=-1, keepdims=True) + RMS_EPS) * g


def _rot_half(x):
    pieces = []
    for i in range(x.shape[1] // 128):
        p = x[:, 128 * i:128 * (i + 1)]
        lane = lax.broadcasted_iota(jnp.int32, p.shape, 1)
        pieces.append(jnp.where((lane & 63) < 32, pltpu.roll(p, 96, 1), pltpu.roll(p, 32, 1)))
    return jnp.concatenate(pieces, axis=1)


def _in_kernel(x_ref, g_ref, w_ref, wgt_ref, pm_ref, pr_ref, pw_ref, gt_ref):
    hb = _bf(_rms(x_ref[...], g_ref[...]))
    pm_ref[...] = jnp.dot(hb, w_ref[:, 0:PM_W], preferred_element_type=F32)
    pr_ref[...] = jnp.dot(hb, w_ref[:, PM_W:PM_W + R_COLS], preferred_element_type=F32)
    pw_ref[...] = jnp.dot(hb, w_ref[:, PM_W + R_COLS:P_W], preferred_element_type=F32)
    gt_ref[...] = lax.dot_general(_bf(wgt_ref[...]), hb, (((1,), (1,)), ((), ())),
                                  preferred_element_type=F32)


def _layer_spec(arr, l, grid_rank):
    zeros = (0,) * (arr.ndim - 1)
    if grid_rank == 1:
        return pl.BlockSpec((None,) + arr.shape[1:], lambda i: (l,) + zeros)
    return pl.BlockSpec((None,) + arr.shape[1:], lambda b, c: (l,) + zeros)


def _in_proj(x2, g_all, w_pad_all, wgt_all, l, tm):
    n, d = x2.shape
    return pl.pallas_call(
        _in_kernel,
        grid=(n // tm,),
        in_specs=[pl.BlockSpec((tm, d), lambda i: (i, 0)),
                  _layer_spec(g_all, l, 1),
                  _layer_spec(w_pad_all, l, 1),
                  _layer_spec(wgt_all, l, 1)],
        out_specs=[pl.BlockSpec((tm, PM_W), lambda i: (i, 0)),
                   pl.BlockSpec((tm, R_COLS), lambda i: (i, 0)),
                   pl.BlockSpec((tm, W_COLS), lambda i: (i, 0)),
                   pl.BlockSpec((GT_ROWS, tm), lambda i: (0, i))],
        out_shape=[jax.ShapeDtypeStruct((n, PM_W), F32),
                   jax.ShapeDtypeStruct((n, R_COLS), F32),
                   jax.ShapeDtypeStruct((n, W_COLS), F32),
                   jax.ShapeDtypeStruct((GT_ROWS, n), F32)],
        compiler_params=pltpu.CompilerParams(dimension_semantics=("arbitrary",),
                                             vmem_limit_bytes=VMEM_LIMIT),
        name="in_proj",
    )(x2, g_all, w_pad_all, wgt_all)


def _rows(c):
    return slice(c * CHUNK, (c + 1) * CHUNK)


def _grows(g):
    return slice(g * GROUP * CHUNK, (g + 1) * GROUP * CHUNK)


def _lanes(h):
    return slice(h * HEAD_DIM, (h + 1) * HEAD_DIM)


def _shift_rows(cur, prev_tail, j):
    rolled = pltpu.roll(cur, j, 0)
    row = lax.broadcasted_iota(jnp.int32, prev_tail.shape, 0)
    head = jnp.where(row >= j, rolled[0:TAIL, :], pltpu.roll(prev_tail, j, 0))
    return jnp.concatenate([head, rolled[TAIL:, :]], axis=0)


def _cummax_rows(x):
    row = lax.broadcasted_iota(jnp.int32, x.shape, 0)
    step = 1
    while step < x.shape[0]:
        x = jnp.maximum(x, jnp.where(row >= step, pltpu.roll(x, step, 0), -jnp.inf))
        step *= 2
    return x


def _head_sums(x, bd_bf, signed):
    outs = []
    off = 0
    while off < x.shape[1]:
        n = min(BD_W, x.shape[1] - off)
        b = bd_bf[0:n, 0:n]
        if signed:
            hi, lo = _split2(x[:, off:off + n])
            outs.append(jnp.dot(hi, b, preferred_element_type=F32) + jnp.dot(lo, b, preferred_element_type=F32))
        else:
            outs.append(jnp.dot(_bf(x[:, off:off + n]), b, preferred_element_type=F32))
        off += n
    return jnp.concatenate(outs, axis=1)


def _heads_ln(x, eps, bd_bf):
    mu = _head_sums(x, bd_bf, True) * (1.0 / HEAD_DIM)
    xc = x - mu
    var = _head_sums(xc * xc, bd_bf, False) * (1.0 / HEAD_DIM)
    return xc * lax.rsqrt(var + eps)


def _mlstm_heads(pm, gt, prev_tail, conv_w, gate_bias, b_i_col, b_f_col, cn_scr, m_scr, pre_ref,
                 tril_bf, triu_g_bf, causal_g, nch):
    chunks = range(nch)
    groups = range(nch // GROUP)
    qk_raw = pm[:, 0:2 * M_WIDTH]
    acc = qk_raw * conv_w[3:4, :]
    for j in (1, 2, 3):
        acc = acc + _shift_rows(qk_raw, prev_tail, j) * conv_w[3 - j:4 - j, :]
    qk = _silu(acc)
    q = qk[:, 0:M_WIDTH]
    k = qk[:, M_WIDTH:2 * M_WIDTH] * (HEAD_DIM ** -0.5)
    v = pm[:, 2 * M_WIDTH:3 * M_WIDTH]
    g128 = pm[:, 4 * M_WIDTH:4 * M_WIDTH + 128] + gate_bias
    lf128 = _log_sigmoid(g128)
    m = m_scr[...]
    b_al, cm, m_old, m_new, b_last = [], [], [], [], []
    for c in chunks:
        b_c = pltpu.roll(_dot_exact_rhs(tril_bf, lf128[_rows(c), :]), 128 - M_HEADS, 1)
        cm_c = _cummax_rows(g128[_rows(c), :] - b_c)
        bl = b_c[CHUNK - 1:CHUNK, :]
        b_al.append(b_c)
        cm.append(cm_c)
        b_last.append(bl)
        m_old.append(m)
        m = bl + jnp.maximum(m, cm_c[CHUNK - 1:CHUNK, :])
        m_new.append(m)
    m_scr[...] = m
    cat = lambda xs: jnp.concatenate(xs, axis=0)
    top = [jnp.maximum(m_old[c], cm[c]) for c in chunks]
    neg_top = cat([-t for t in top])
    inter_w = cat([jnp.exp(m_old[c] - top[c]) for c in chunks])
    e_mrow = cat([jnp.exp(-(b_al[c] + top[c])) for c in chunks])
    sw = cat([jnp.exp(b_last[c] - b_al[c] + g128[_rows(c), :] - m_new[c]) for c in chunks])
    dec = [jnp.exp(b_last[c] + m_old[c] - m_new[c]) for c in chunks]
    d_row = []
    for g in groups:
        i_r = gt[0:M_HEADS, _grows(g)] + b_i_col
        lf_r = _log_sigmoid(gt[M_HEADS:2 * M_HEADS, _grows(g)] + b_f_col)
        d_row.append(i_r - _dot_exact_lhs(lf_r, triu_g_bf))
    ones = jnp.ones((GROUP * CHUNK, HEAD_DIM), F32)

    def head(h):
        sl = _lanes(h)
        hc = slice(h, h + 1)
        qk_t = [_dot_nt(q[_grows(g), sl], k[_grows(g), sl]) for g in groups]
        vaug = [jnp.concatenate([v[_grows(g), sl], ones], axis=1) for g in groups]
        cn_in = [_dot_tn(k[_rows(c), sl] * sw[_rows(c), hc],
                         vaug[c // GROUP][_rows(c % GROUP), :]) for c in chunks]
        yield
        e = [jnp.exp(jnp.where(causal_g, neg_top[_grows(g), hc] + d_row[g][hc, :], -jnp.inf))
             for g in groups]
        sva = [_dot(qk_t[g] * e[g], vaug[g]) for g in groups]
        cn = cn_scr[h]
        qcn = []
        for c in chunks:
            qcn.append(_dot(q[_rows(c), sl], cn))
            cn = dec[c][:, hc] * cn + cn_in[c]
        cn_scr[h] = cn
        yield
        for g in groups:
            nd = sva[g] + inter_w[_grows(g), hc] * jnp.concatenate(qcn[g * GROUP:(g + 1) * GROUP], axis=0)
            pre_ref[_grows(g), sl] = nd[:, 0:HEAD_DIM] / jnp.maximum(
                jnp.abs(nd[:, HEAD_DIM:2 * HEAD_DIM]), e_mrow[_grows(g), hc])

    return [head(h) for h in range(M_HEADS)], qk_raw


def _mlstm_finish(pm, pre, ln_g, bd_bf):
    o = pm[:, 3 * M_WIDTH:4 * M_WIDTH]
    return _sigmoid(o) * (_heads_ln(pre, HEAD_LN_EPS, bd_bf) * ln_g)


def _retention_heads(pr, cos, sin, dmat_ref, qdec, kdec, cdec, r_scr, pre_ref, nch):
    chunks = range(nch)
    groups = range(nch // GROUP)
    q = pr[:, 0:R_WIDTH]
    k = pr[:, R_WIDTH:2 * R_WIDTH]
    v = pr[:, 2 * R_WIDTH:3 * R_WIDTH]
    qr = q * cos + _rot_half(q) * sin
    kr = (k * cos + _rot_half(k) * sin) * (HEAD_DIM ** -0.5)
    qd = qr * qdec
    kd = kr * kdec

    def head(h):
        sl = _lanes(h)
        qk_t = [_dot_nt(qr[_grows(g), sl], kr[_grows(g), sl]) for g in groups]
        r_in = [_dot_tn(kd[_rows(c), sl], v[_rows(c), sl]) for c in chunks]
        yield
        intra = [_dot(qk_t[g] * dmat_ref[h], v[_grows(g), sl]) for g in groups]
        r_st = r_scr[h]
        cross = []
        for c in chunks:
            cross.append(_dot(qd[_rows(c), sl], r_st))
            r_st = cdec[:, sl] * r_st + r_in[c]
        r_scr[h] = r_st
        yield
        for g in groups:
            pre_ref[_grows(g), sl] = intra[g] + jnp.concatenate(cross[g * GROUP:(g + 1) * GROUP], axis=0)

    return [head(h) for h in range(R_HEADS)]


def _retention_finish(pr, pre, bd_bf):
    return _silu(pr[:, 3 * R_WIDTH:4 * R_WIDTH]) * _heads_ln(pre, HEAD_LN_EPS, bd_bf)


def _rwkv_heads(pw, prev_tail, mu, w0, w_up, a0, a_up, g_up, k_k, k_a, r_k,
                s_scr, pre_ref, tril_bf, bd_bf, lower_incl2, lower_strict2, nch):
    chunks = range(nch)
    groups = range(nch // GROUP)
    rg_n = GROUP * CHUNK
    ps = pw + mu * (_shift_rows(pw, prev_tail, 1) - pw)
    r = ps[:, 0:W_WIDTH]
    k = ps[:, W_WIDTH:2 * W_WIDTH]
    v = ps[:, 2 * W_WIDTH:3 * W_WIDTH]
    o1 = 3 * W_WIDTH
    wl = ps[:, o1:o1 + W_LORA]
    al = ps[:, o1 + W_LORA:o1 + W_LORA + A_LORA]
    gl = ps[:, o1 + W_LORA + A_LORA:o1 + W_LORA + A_LORA + G_LORA]
    w_log = -_softplus(-(w0 + _dot(jnp.tanh(wl), w_up))) - 0.5
    lw = -jnp.exp(w_log)
    a = _sigmoid(a0 + _dot(al, a_up))
    g = _dot(_sigmoid(gl), g_up)
    kk0 = k * k_k
    kk = kk0 / jnp.maximum(jnp.sqrt(_head_sums(kk0 * kk0, bd_bf, False)), 1e-12)
    k2 = k * (1.0 + (a - 1.0) * k_a)
    cum_c = [_dot_exact_rhs(tril_bf, lw[_rows(c), :]) for c in chunks]
    cum = jnp.concatenate(cum_c, axis=0)
    to_end = jnp.concatenate([cum_c[c][CHUNK - 1:CHUNK, :] - cum_c[c] for c in chunks], axis=0)
    e_pos = jnp.exp(cum)
    e_neg = jnp.exp(-cum)
    e_last = jnp.exp(to_end)
    b_t = -(kk * a)
    a_prev = kk * jnp.exp(cum - lw)
    b_neg = b_t * e_neg
    b_end = b_t * e_last
    r_pos = r * e_pos
    k_neg = k2 * e_neg
    k_end = k2 * e_last
    bonus_v = _head_sums(r * k2 * r_k, bd_bf, True) * v

    def head(h):
        sl = _lanes(h)
        ar = [jnp.concatenate([a_prev[_grows(q_), sl], r_pos[_grows(q_), sl]], axis=0) for q_ in groups]
        bk = [jnp.concatenate([b_neg[_grows(q_), sl], k_neg[_grows(q_), sl]], axis=0) for q_ in groups]
        vg = [v[_grows(q_), sl] for q_ in groups]
        m = [_dot_nt(ar[q_], bk[q_]) for q_ in groups]
        yield
        top = [jnp.where(lower_strict2, m[q_][0:rg_n, :], 0.0) for q_ in groups]
        bot = [jnp.where(lower_incl2, m[q_][rg_n:2 * rg_n, :], 0.0) for q_ in groups]
        p = [top[q_][:, 0:rg_n] for q_ in groups]
        akv = [_dot(top[q_][:, rg_n:2 * rg_n], vg[q_]) for q_ in groups]
        yield
        x = [jnp.concatenate([ar[q_][0:rg_n, :], akv[q_]], axis=1) for q_ in groups]
        for _ in range(5):
            px = [_dot(p[q_], jnp.concatenate([x[q_], p[q_]], axis=1)) for q_ in groups]
            x = [x[q_] + px[q_][:, 0:2 * HEAD_DIM] for q_ in groups]
            p = [px[q_][:, 2 * HEAD_DIM:2 * HEAD_DIM + rg_n] for q_ in groups]
            yield
        x = [x[q_] + _dot(p[q_], x[q_]) for q_ in groups]
        yield
        s_st = s_scr[h]
        for q_ in groups:
            us, rss = [], []
            for j in range(GROUP):
                c = q_ * GROUP + j
                rs = _rows(c)
                xj = x[q_][_rows(j), :]
                wr = jnp.concatenate([xj[:, 0:HEAD_DIM], r_pos[rs, sl]], axis=0)
                gs = _dot_nt(wr, s_st)
                yield
                u = gs[0:CHUNK, :] + xj[:, HEAD_DIM:2 * HEAD_DIM]
                uv = jnp.concatenate([u, v[rs, sl]], axis=0)
                bk_end = jnp.concatenate([b_end[rs, sl], k_end[rs, sl]], axis=0)
                s_st = s_st * e_pos[(c + 1) * CHUNK - 1:(c + 1) * CHUNK, sl] + _dot_tn(uv, bk_end)
                us.append(u)
                rss.append(gs[CHUNK:2 * CHUNK, :])
                yield
            uvg = jnp.concatenate(us + [vg[q_]], axis=0)
            pre_ref[_grows(q_), sl] = jnp.concatenate(rss, axis=0) + _dot(bot[q_], uvg)
        s_scr[h] = s_st

    return [head(h) for h in range(W_HEADS)], (g, bonus_v)


def _rwkv_finish(pre, g, bonus_v, ln_g, ln_b, bd_bf):
    return (_heads_ln(pre, RWKV_LN_EPS, bd_bf) * ln_g + ln_b + bonus_v) * g


def _run_interleaved(gens):
    pending = list(gens)
    while pending:
        for g in list(pending):
            try:
                next(g)
            except StopIteration:
                pending.remove(g)


def _mix_kernel(pm_ref, pr_ref, pw_ref, gt_ref, cos_ref, sin_ref, qdec_ref, kdec_ref, dmat_ref,
                cdec_ref, conv_ref, gbias_ref, bi_col_ref, bf_col_ref, mln_ref,
                mu_ref, w0_ref, wup_ref, a0_ref, aup_ref, gup_ref, kk_ref, ka_ref, rk_ref,
                wlng_ref, wlnb_ref,
                y_ref,
                cn_scr, m_scr, r_scr, s_scr, prevqk_scr, prevpw_scr, prem_scr, prer_scr, prew_scr,
                *, nch):
    @pl.when(pl.program_id(1) == 0)
    def _():
        cn_scr[...] = jnp.zeros(cn_scr.shape, F32)
        m_scr[...] = jnp.full(m_scr.shape, M_INIT, F32)
        r_scr[...] = jnp.zeros(r_scr.shape, F32)
        s_scr[...] = jnp.zeros(s_scr.shape, F32)
        prevqk_scr[...] = jnp.zeros(prevqk_scr.shape, F32)
        prevpw_scr[...] = jnp.zeros(prevpw_scr.shape, F32)

    rg_n = GROUP * CHUNK
    row = lax.broadcasted_iota(jnp.int32, (CHUNK, CHUNK), 0)
    col = lax.broadcasted_iota(jnp.int32, (CHUNK, CHUNK), 1)
    tril_bf = jnp.where(col <= row, 1.0, 0.0).astype(BF16)
    rowb = lax.broadcasted_iota(jnp.int32, (BD_W, BD_W), 0)
    colb = lax.broadcasted_iota(jnp.int32, (BD_W, BD_W), 1)
    bd_bf = jnp.where((rowb >> 6) == (colb >> 6), 1.0, 0.0).astype(BF16)
    row2 = lax.broadcasted_iota(jnp.int32, (rg_n, 2 * rg_n), 0)
    col2 = lax.broadcasted_iota(jnp.int32, (rg_n, 2 * rg_n), 1) & (rg_n - 1)
    start2 = (row2 >> 6) << 6
    lower_incl2 = (col2 <= row2) & (col2 >= start2)
    lower_strict2 = (col2 < row2) & (col2 >= start2)
    causal_g = lower_incl2[:, 0:rg_n]
    rowg = lax.broadcasted_iota(jnp.int32, (rg_n, rg_n), 0)
    colg = lax.broadcasted_iota(jnp.int32, (rg_n, rg_n), 1)
    triu_g_bf = jnp.where((rowg <= colg) & (rowg >= ((colg >> 6) << 6)), 1.0, 0.0).astype(BF16)

    tt = nch * CHUNK
    pw = pw_ref[0]
    w_gens, (w_gate, bonus_v) = _rwkv_heads(
        pw, prevpw_scr[...], mu_ref[...], w0_ref[...], wup_ref[...], a0_ref[...], aup_ref[...],
        gup_ref[...], kk_ref[...], ka_ref[...], rk_ref[...], s_scr, prew_scr, tril_bf, bd_bf,
        lower_incl2, lower_strict2, nch)
    prevpw_scr[...] = pw[tt - TAIL:tt, :]
    pm = pm_ref[0]
    m_gens, qk_raw = _mlstm_heads(pm, gt_ref[...], prevqk_scr[...], conv_ref[...], gbias_ref[...],
                                  bi_col_ref[...], bf_col_ref[...], cn_scr, m_scr, prem_scr,
                                  tril_bf, triu_g_bf, causal_g, nch)
    prevqk_scr[...] = qk_raw[tt - TAIL:tt, :]
    pr = pr_ref[0]
    r_gens = _retention_heads(pr, cos_ref[...], sin_ref[...], dmat_ref, qdec_ref[...],
                              kdec_ref[...], cdec_ref[...], r_scr, prer_scr, nch)
    _run_interleaved(w_gens + m_gens + r_gens)

    y_ref[0, :, 0:M_WIDTH] = _mlstm_finish(pm, prem_scr[...], mln_ref[...], bd_bf).astype(y_ref.dtype)
    y_ref[0, :, M_WIDTH:M_WIDTH + R_WIDTH] = _retention_finish(pr, prer_scr[...], bd_bf).astype(y_ref.dtype)
    y_ref[0, :, M_WIDTH + R_WIDTH:M_WIDTH + R_WIDTH + W_WIDTH] = _rwkv_finish(
        prew_scr[...], w_gate, bonus_v, wlng_ref[...], wlnb_ref[...], bd_bf).astype(y_ref.dtype)


def _full(shape):
    nd = len(shape)
    return pl.BlockSpec(shape, lambda b, c: (0,) * nd)


def _mixers(pm, pr, pw, gt, tabs, layer_params, l, nch):
    bsz, t, _ = pm.shape
    tt = nch * CHUNK
    nblk = t // tt
    tab_args = list(tabs)
    args = [pm, pr, pw, gt] + tab_args + list(layer_params)
    in_specs = [
        pl.BlockSpec((1, tt, PM_W), lambda b, c: (b, c, 0)),
        pl.BlockSpec((1, tt, R_COLS), lambda b, c: (b, c, 0)),
        pl.BlockSpec((1, tt, W_COLS), lambda b, c: (b, c, 0)),
        pl.BlockSpec((GT_ROWS, tt), lambda b, c: (0, b * nblk + c)),
    ] + [pl.BlockSpec((tt, R_WIDTH), lambda b, c: (c, 0)) for _ in range(4)] + [
        _full(a.shape) for a in tab_args[4:]] + [_layer_spec(a, l, 2) for a in layer_params]
    d_out = M_WIDTH + R_WIDTH + W_WIDTH
    return pl.pallas_call(
        functools.partial(_mix_kernel, nch=nch),
        grid=(bsz, t // tt),
        in_specs=in_specs,
        out_specs=pl.BlockSpec((1, tt, d_out), lambda b, c: (b, c, 0)),
        out_shape=jax.ShapeDtypeStruct((bsz, t, d_out), BF16),
        scratch_shapes=[
            pltpu.VMEM((M_HEADS, HEAD_DIM, 2 * HEAD_DIM), F32),
            pltpu.VMEM((1, 128), F32),
            pltpu.VMEM((R_HEADS, HEAD_DIM, HEAD_DIM), F32),
            pltpu.VMEM((W_HEADS, HEAD_DIM, HEAD_DIM), F32),
            pltpu.VMEM((TAIL, 2 * M_WIDTH), F32),
            pltpu.VMEM((TAIL, W_COLS), F32),
            pltpu.VMEM((tt, M_WIDTH), F32),
            pltpu.VMEM((tt, R_WIDTH), F32),
            pltpu.VMEM((tt, W_WIDTH), F32),
        ],
        compiler_params=pltpu.CompilerParams(dimension_semantics=("arbitrary", "arbitrary"),
                                             vmem_limit_bytes=VMEM_LIMIT),
        name="mixers",
    )(*args)


def _ffn_kernel(y_ref, wo_ref, x_ref, g_ref, wg_ref, wu_ref, wd_ref, gf_ref, o_ref, *, ff_chunk, final_norm):
    x = x_ref[...] + jnp.dot(y_ref[...], wo_ref[...], preferred_element_type=F32)
    hb = _bf(_rms(x, g_ref[...]))
    acc = x
    d_ff = wg_ref.shape[1]
    for c in range(d_ff // ff_chunk):
        cs = slice(c * ff_chunk, (c + 1) * ff_chunk)
        gate = jnp.dot(hb, wg_ref[:, cs], preferred_element_type=F32)
        up = jnp.dot(hb, wu_ref[:, cs], preferred_element_type=F32)
        acc = acc + jnp.dot(_bf(_silu(gate) * up), wd_ref[cs, :], preferred_element_type=F32)
    if final_norm:
        acc = _rms(acc, gf_ref[...])
    o_ref[...] = acc


def _out_ffn(y2, x2, wo_all, g_all, wg_all, wu_all, wd_all, gfin, l, tm, final_norm):
    n, d = x2.shape
    kern = functools.partial(_ffn_kernel, ff_chunk=FF_CHUNK, final_norm=final_norm)
    return pl.pallas_call(
        kern,
        grid=(n // tm,),
        in_specs=[pl.BlockSpec((tm, y2.shape[1]), lambda i: (i, 0)),
                  _layer_spec(wo_all, l, 1),
                  pl.BlockSpec((tm, d), lambda i: (i, 0)),
                  _layer_spec(g_all, l, 1),
                  _layer_spec(wg_all, l, 1),
                  _layer_spec(wu_all, l, 1),
                  _layer_spec(wd_all, l, 1),
                  pl.BlockSpec((1, d), lambda i: (0, 0))],
        out_specs=pl.BlockSpec((tm, d), lambda i: (i, 0)),
        out_shape=jax.ShapeDtypeStruct((n, d), F32),
        compiler_params=pltpu.CompilerParams(dimension_semantics=("arbitrary",),
                                             vmem_limit_bytes=VMEM_LIMIT),
        name="out_ffn",
    )(y2, wo_all, x2, g_all, wg_all, wu_all, wd_all, gfin)


def _retention_tables(t):
    h, d = R_HEADS, HEAD_DIM
    pos = jnp.arange(t, dtype=F32)
    theta = 1.0 / (ROPE_BASE ** jnp.linspace(0.0, 1.0, d // 2, dtype=F32))
    ang = pos[:, None] * theta[None, :]
    cos, sin = jnp.cos(ang), jnp.sin(ang)
    cos_t = jnp.tile(jnp.concatenate([cos, cos], axis=1), (1, h))
    sin_t = jnp.tile(jnp.concatenate([-sin, sin], axis=1), (1, h))
    log_g = jnp.log(1.0 - 2.0 ** (-5.0 - jnp.arange(h, dtype=F32)))
    idx = jnp.arange(CHUNK, dtype=F32)
    diff = idx[:, None] - idx[None, :]
    dmat = jnp.where(diff >= 0, jnp.exp(jnp.maximum(diff, 0.0)[None] * log_g[:, None, None]), 0.0)
    q_dec = jnp.exp((idx[None, :] + 1.0) * log_g[:, None])
    k_dec = jnp.exp((CHUNK - 1.0 - idx[None, :]) * log_g[:, None])
    c_dec = jnp.exp(CHUNK * log_g)
    expand = lambda z: jnp.repeat(z.T, d, axis=1)
    eye_g = jnp.eye(GROUP, dtype=F32)
    dmat_g = jnp.einsum('ab,hij->haibj', eye_g, dmat).reshape(h, GROUP * CHUNK, GROUP * CHUNK)
    reps = t // CHUNK
    return (cos_t, sin_t, jnp.tile(expand(q_dec), (reps, 1)), jnp.tile(expand(k_dec), (reps, 1)),
            dmat_g, jnp.repeat(c_dec, d)[None, :])


def kernel(x, w_in, ln1_g, ln2_g, lnf_g, m_conv, m_b_i, m_b_f, m_ln_g, rw_mu, rw_w0, rw_w_up,
           rw_a0, rw_a_up, rw_g_up, rw_k_k, rw_k_a, rw_r_k, rw_ln_g, rw_ln_b, w_out, w_gate,
           w_up, w_down):
    bsz, t, d = x.shape
    depth = w_in.shape[0]
    n = bsz * t
    nc = t // CHUNK
    tm = min(512, n)
    nch = math.gcd(nc, NCH)
    tabs = _retention_tables(t)
    x2 = x.reshape(n, d)
    rows3 = lambda z: z.reshape(depth, 1, -1)
    cols3 = lambda z: z.reshape(depth, -1, 1)
    w_pad = jnp.concatenate(
        [w_in[:, :, :M_COLS], jnp.zeros((depth, d, PM_W - M_COLS), w_in.dtype), w_in[:, :, M_COLS:]],
        axis=2).astype(BF16)
    gate_w = w_in[:, :, 4 * M_WIDTH:M_COLS].transpose(0, 2, 1)
    gate_w = jnp.concatenate([gate_w, jnp.zeros((depth, GT_ROWS - 2 * M_HEADS, d), F32)], axis=1)
    gate_bias = jnp.concatenate(
        [m_b_i, m_b_f, jnp.zeros((depth, 128 - 2 * M_HEADS), F32)], axis=1)[:, None, :]
    layer_params = (m_conv, gate_bias, cols3(m_b_i), cols3(m_b_f), rows3(m_ln_g), rows3(rw_mu),
                    rows3(rw_w0), rw_w_up, rows3(rw_a0), rw_a_up, rw_g_up, rows3(rw_k_k),
                    rows3(rw_k_a), rows3(rw_r_k), rows3(rw_ln_g), rows3(rw_ln_b))
    wo_bf, wg_bf, wu_bf, wd_bf = (w.astype(BF16) for w in (w_out, w_gate, w_up, w_down))
    ln1, ln2 = rows3(ln1_g), rows3(ln2_g)
    for l in range(depth):
        pm, pr, pw, gt = _in_proj(x2, ln1, w_pad, gate_w, l, tm)
        y = _mixers(pm.reshape(bsz, t, PM_W), pr.reshape(bsz, t, R_COLS), pw.reshape(bsz, t, W_COLS),
                    gt, tabs, layer_params, l, nch)
        x2 = _out_ffn(y.reshape(n, d), x2, wo_bf, ln2, wg_bf, wu_bf, wd_bf, lnf_g.reshape(1, -1), l, tm,
                      final_norm=(l == depth - 1))
    return x2.reshape(bsz, t, d)
```

```python
import functools
import math

import jax
import jax.numpy as jnp
from jax import lax
from jax.experimental import pallas as pl
from jax.experimental.pallas import tpu as pltpu

F32 = jnp.float32
BF16 = jnp.bfloat16

CHUNK = 64
HEAD_DIM = 64
M_HEADS, R_HEADS, W_HEADS = 4, 6, 6
M_WIDTH, R_WIDTH, W_WIDTH = 256, 384, 384
W_LORA, A_LORA, G_LORA = 64, 64, 128
M_COLS = 4 * M_WIDTH + 2 * M_HEADS
R_COLS = 4 * R_WIDTH
W_COLS = 3 * W_WIDTH + W_LORA + A_LORA + G_LORA
PM_W = 4 * M_WIDTH + 128
P_W = PM_W + R_COLS + W_COLS
RMS_EPS = 1e-6
HEAD_LN_EPS = 1e-5
RWKV_LN_EPS = 64e-5
ROPE_BASE = 10000.0
M_INIT = -1e30
VMEM_LIMIT = 56 * 1024 * 1024
TAIL = 8
NCH = 8
GROUP = 2
BD_W = 256
FF_CHUNK = 256
GT_ROWS = 8


def _bf(x):
    return x.astype(BF16)


def _dot(a, b):
    return jnp.dot(_bf(a), _bf(b), preferred_element_type=F32)


def _dot_nt(a, b):
    return lax.dot_general(_bf(a), _bf(b), (((1,), (1,)), ((), ())), preferred_element_type=F32)


def _dot_tn(a, b):
    return lax.dot_general(_bf(a), _bf(b), (((0,), (0,)), ((), ())), preferred_element_type=F32)


def _split2(x):
    hi = _bf(x)
    lo = _bf(x - hi.astype(F32))
    return hi, lo


def _split3(x):
    x1 = _bf(x)
    r = x - x1.astype(F32)
    x2 = _bf(r)
    x3 = _bf(r - x2.astype(F32))
    return x1, x2, x3


def _dot_exact_rhs(m_bf, x):
    x1, x2, x3 = _split3(x)
    return (jnp.dot(m_bf, x1, preferred_element_type=F32)
            + jnp.dot(m_bf, x2, preferred_element_type=F32)
            + jnp.dot(m_bf, x3, preferred_element_type=F32))


def _dot_exact_lhs(x, m_bf):
    x1, x2, x3 = _split3(x)
    return (jnp.dot(x1, m_bf, preferred_element_type=F32)
            + jnp.dot(x2, m_bf, preferred_element_type=F32)
            + jnp.dot(x3, m_bf, preferred_element_type=F32))


def _sigmoid(x):
    return 1.0 / (1.0 + jnp.exp(-x))


def _silu(x):
    return x * _sigmoid(x)


def _softplus(x):
    return jnp.maximum(x, 0.0) + jnp.log(1.0 + jnp.exp(-jnp.abs(x)))


def _log_sigmoid(x):
    return -_softplus(-x)


def _rms(x, g):
    return x * lax.rsqrt(jnp.mean(x * x, axis=-1, keepdims=True) + RMS_EPS) * g


def _rot_half(x):
    pieces = []
    for i in range(x.shape[1] // 128):
        p = x[:, 128 * i:128 * (i + 1)]
        lane = lax.broadcasted_iota(jnp.int32, p.shape, 1)
        pieces.append(jnp.where((lane & 63) < 32, pltpu.roll(p, 96, 1), pltpu.roll(p, 32, 1)))
    return jnp.concatenate(pieces, axis=1)


def _in_kernel(x_ref, g_ref, w_ref, pm_ref, pr_ref, pw_ref, gt_ref):
    hb = _bf(_rms(x_ref[...], g_ref[...]))
    p = jnp.dot(hb, w_ref[...], preferred_element_type=F32)
    pm_ref[...] = p[:, 0:PM_W]
    pr_ref[...] = p[:, PM_W:PM_W + R_COLS]
    pw_ref[...] = p[:, PM_W + R_COLS:P_W]
    gt_ref[...] = jnp.transpose(p[:, 4 * M_WIDTH:PM_W])[0:GT_ROWS, :]


def _layer_spec(arr, l, grid_rank):
    zeros = (0,) * (arr.ndim - 1)
    if grid_rank == 1:
        return pl.BlockSpec((None,) + arr.shape[1:], lambda i: (l,) + zeros)
    return pl.BlockSpec((None,) + arr.shape[1:], lambda b, c: (l,) + zeros)


def _in_proj(x2, g_all, w_pad_all, l, tm):
    n, d = x2.shape
    return pl.pallas_call(
        _in_kernel,
        grid=(n // tm,),
        in_specs=[pl.BlockSpec((tm, d), lambda i: (i, 0)),
                  _layer_spec(g_all, l, 1),
                  _layer_spec(w_pad_all, l, 1)],
        out_specs=[pl.BlockSpec((tm, PM_W), lambda i: (i, 0)),
                   pl.BlockSpec((tm, R_COLS), lambda i: (i, 0)),
                   pl.BlockSpec((tm, W_COLS), lambda i: (i, 0)),
                   pl.BlockSpec((GT_ROWS, tm), lambda i: (0, i))],
        out_shape=[jax.ShapeDtypeStruct((n, PM_W), F32),
                   jax.ShapeDtypeStruct((n, R_COLS), F32),
                   jax.ShapeDtypeStruct((n, W_COLS), F32),
                   jax.ShapeDtypeStruct((GT_ROWS, n), F32)],
        compiler_params=pltpu.CompilerParams(dimension_semantics=("arbitrary",),
                                             vmem_limit_bytes=VMEM_LIMIT),
        name="in_proj",
    )(x2, g_all, w_pad_all)


def _rows(c):
    return slice(c * CHUNK, (c + 1) * CHUNK)


def _grows(g):
    return slice(g * GROUP * CHUNK, (g + 1) * GROUP * CHUNK)


def _lanes(h):
    return slice(h * HEAD_DIM, (h + 1) * HEAD_DIM)


def _shift_rows(cur, prev_tail, j):
    rolled = pltpu.roll(cur, j, 0)
    row = lax.broadcasted_iota(jnp.int32, prev_tail.shape, 0)
    head = jnp.where(row >= j, rolled[0:TAIL, :], pltpu.roll(prev_tail, j, 0))
    return jnp.concatenate([head, rolled[TAIL:, :]], axis=0)


def _cummax_rows(x):
    row = lax.broadcasted_iota(jnp.int32, x.shape, 0)
    step = 1
    while step < x.shape[0]:
        x = jnp.maximum(x, jnp.where(row >= step, pltpu.roll(x, step, 0), -jnp.inf))
        step *= 2
    return x


def _head_sums(x, bd_bf, signed):
    outs = []
    off = 0
    while off < x.shape[1]:
        n = min(BD_W, x.shape[1] - off)
        b = bd_bf[0:n, 0:n]
        if signed:
            hi, lo = _split2(x[:, off:off + n])
            outs.append(jnp.dot(hi, b, preferred_element_type=F32) + jnp.dot(lo, b, preferred_element_type=F32))
        else:
            outs.append(jnp.dot(_bf(x[:, off:off + n]), b, preferred_element_type=F32))
        off += n
    return jnp.concatenate(outs, axis=1)


def _heads_ln(x, eps, bd_bf):
    mu = _head_sums(x, bd_bf, True) * (1.0 / HEAD_DIM)
    xc = x - mu
    var = _head_sums(xc * xc, bd_bf, False) * (1.0 / HEAD_DIM)
    return xc * lax.rsqrt(var + eps)


def _mlstm_heads(pm, gt, prev_tail, conv_w, gate_bias, b_i_col, b_f_col, cn_scr, m_scr, pre_ref,
                 tril_bf, triu_g_bf, causal_g, nch):
    chunks = range(nch)
    groups = range(nch // GROUP)
    qk_raw = pm[:, 0:2 * M_WIDTH]
    acc = qk_raw * conv_w[3:4, :]
    for j in (1, 2, 3):
        acc = acc + _shift_rows(qk_raw, prev_tail, j) * conv_w[3 - j:4 - j, :]
    qk = _silu(acc)
    q = qk[:, 0:M_WIDTH]
    k = qk[:, M_WIDTH:2 * M_WIDTH] * (HEAD_DIM ** -0.5)
    v = pm[:, 2 * M_WIDTH:3 * M_WIDTH]
    g128 = pm[:, 4 * M_WIDTH:4 * M_WIDTH + 128] + gate_bias
    lf128 = _log_sigmoid(g128)
    m = m_scr[...]
    b_al, cm, m_old, m_new, b_last = [], [], [], [], []
    for c in chunks:
        b_c = pltpu.roll(_dot_exact_rhs(tril_bf, lf128[_rows(c), :]), 128 - M_HEADS, 1)
        cm_c = _cummax_rows(g128[_rows(c), :] - b_c)
        bl = b_c[CHUNK - 1:CHUNK, :]
        b_al.append(b_c)
        cm.append(cm_c)
        b_last.append(bl)
        m_old.append(m)
        m = bl + jnp.maximum(m, cm_c[CHUNK - 1:CHUNK, :])
        m_new.append(m)
    m_scr[...] = m
    cat = lambda xs: jnp.concatenate(xs, axis=0)
    top = [jnp.maximum(m_old[c], cm[c]) for c in chunks]
    neg_top = cat([-t for t in top])
    inter_w = cat([jnp.exp(m_old[c] - top[c]) for c in chunks])
    e_mrow = cat([jnp.exp(-(b_al[c] + top[c])) for c in chunks])
    sw = cat([jnp.exp(b_last[c] - b_al[c] + g128[_rows(c), :] - m_new[c]) for c in chunks])
    dec = [jnp.exp(b_last[c] + m_old[c] - m_new[c]) for c in chunks]
    d_row = []
    for g in groups:
        i_r = gt[0:M_HEADS, _grows(g)] + b_i_col
        lf_r = _log_sigmoid(gt[M_HEADS:2 * M_HEADS, _grows(g)] + b_f_col)
        d_row.append(i_r - _dot_exact_lhs(lf_r, triu_g_bf))
    ones = jnp.ones((GROUP * CHUNK, HEAD_DIM), F32)

    def head(h):
        sl = _lanes(h)
        hc = slice(h, h + 1)
        qk_t = [_dot_nt(q[_grows(g), sl], k[_grows(g), sl]) for g in groups]
        vaug = [jnp.concatenate([v[_grows(g), sl], ones], axis=1) for g in groups]
        cn_in = [_dot_tn(k[_rows(c), sl] * sw[_rows(c), hc],
                         vaug[c // GROUP][_rows(c % GROUP), :]) for c in chunks]
        yield
        e = [jnp.exp(jnp.where(causal_g, neg_top[_grows(g), hc] + d_row[g][hc, :], -jnp.inf))
             for g in groups]
        sva = [_dot(qk_t[g] * e[g], vaug[g]) for g in groups]
        cn = cn_scr[h]
        qcn = []
        for c in chunks:
            qcn.append(_dot(q[_rows(c), sl], cn))
            cn = dec[c][:, hc] * cn + cn_in[c]
        cn_scr[h] = cn
        yield
        for g in groups:
            nd = sva[g] + inter_w[_grows(g), hc] * jnp.concatenate(qcn[g * GROUP:(g + 1) * GROUP], axis=0)
            pre_ref[_grows(g), sl] = nd[:, 0:HEAD_DIM] / jnp.maximum(
                jnp.abs(nd[:, HEAD_DIM:2 * HEAD_DIM]), e_mrow[_grows(g), hc])

    return [head(h) for h in range(M_HEADS)], qk_raw


def _mlstm_finish(pm, pre, ln_g, bd_bf):
    o = pm[:, 3 * M_WIDTH:4 * M_WIDTH]
    return _sigmoid(o) * (_heads_ln(pre, HEAD_LN_EPS, bd_bf) * ln_g)


def _retention_heads(pr, cos, sin, dmat_ref, qdec, kdec, cdec, r_scr, pre_ref, nch):
    chunks = range(nch)
    groups = range(nch // GROUP)
    q = pr[:, 0:R_WIDTH]
    k = pr[:, R_WIDTH:2 * R_WIDTH]
    v = pr[:, 2 * R_WIDTH:3 * R_WIDTH]
    qr = q * cos + _rot_half(q) * sin
    kr = (k * cos + _rot_half(k) * sin) * (HEAD_DIM ** -0.5)
    qd = qr * qdec
    kd = kr * kdec

    def head(h):
        sl = _lanes(h)
        qk_t = [_dot_nt(qr[_grows(g), sl], kr[_grows(g), sl]) for g in groups]
        r_in = [_dot_tn(kd[_rows(c), sl], v[_rows(c), sl]) for c in chunks]
        yield
        intra = [_dot(qk_t[g] * dmat_ref[h], v[_grows(g), sl]) for g in groups]
        r_st = r_scr[h]
        cross = []
        for c in chunks:
            cross.append(_dot(qd[_rows(c), sl], r_st))
            r_st = cdec[:, sl] * r_st + r_in[c]
        r_scr[h] = r_st
        yield
        for g in groups:
            pre_ref[_grows(g), sl] = intra[g] + jnp.concatenate(cross[g * GROUP:(g + 1) * GROUP], axis=0)

    return [head(h) for h in range(R_HEADS)]


def _retention_finish(pr, pre, bd_bf):
    return _silu(pr[:, 3 * R_WIDTH:4 * R_WIDTH]) * _heads_ln(pre, HEAD_LN_EPS, bd_bf)


def _rwkv_heads(pw, prev_tail, mu, w0, w_up, a0, a_up, g_up, k_k, k_a, r_k,
                s_scr, pre_ref, tril_bf, bd_bf, lower_incl2, lower_strict2, nch):
    chunks = range(nch)
    groups = range(nch // GROUP)
    rg_n = GROUP * CHUNK
    ps = pw + mu * (_shift_rows(pw, prev_tail, 1) - pw)
    r = ps[:, 0:W_WIDTH]
    k = ps[:, W_WIDTH:2 * W_WIDTH]
    v = ps[:, 2 * W_WIDTH:3 * W_WIDTH]
    o1 = 3 * W_WIDTH
    wl = ps[:, o1:o1 + W_LORA]
    al = ps[:, o1 + W_LORA:o1 + W_LORA + A_LORA]
    gl = ps[:, o1 + W_LORA + A_LORA:o1 + W_LORA + A_LORA + G_LORA]
    lw = -math.exp(-0.5) * _sigmoid(w0 + _dot(jnp.tanh(wl), w_up))
    a = _sigmoid(a0 + _dot(al, a_up))
    g = _dot(_sigmoid(gl), g_up)
    kk0 = k * k_k
    kk = kk0 / jnp.maximum(jnp.sqrt(_head_sums(kk0 * kk0, bd_bf, False)), 1e-12)
    k2 = k * (1.0 + (a - 1.0) * k_a)
    cum_c = [_dot_exact_rhs(tril_bf, lw[_rows(c), :]) for c in chunks]
    cum = jnp.concatenate(cum_c, axis=0)
    to_end = jnp.concatenate([cum_c[c][CHUNK - 1:CHUNK, :] - cum_c[c] for c in chunks], axis=0)
    e_pos = jnp.exp(cum)
    e_neg = jnp.exp(-cum)
    e_last = jnp.exp(to_end)
    b_t = -(kk * a)
    a_prev = kk * jnp.exp(cum - lw)
    b_neg = b_t * e_neg
    b_end = b_t * e_last
    r_pos = r * e_pos
    k_neg = k2 * e_neg
    k_end = k2 * e_last
    bonus_v = _head_sums(r * k2 * r_k, bd_bf, True) * v

    def head(h):
        sl = _lanes(h)
        ar = [jnp.concatenate([a_prev[_grows(q_), sl], r_pos[_grows(q_), sl]], axis=0) for q_ in groups]
        bk = [jnp.concatenate([b_neg[_grows(q_), sl], k_neg[_grows(q_), sl]], axis=0) for q_ in groups]
        vg = [v[_grows(q_), sl] for q_ in groups]
        m = [_dot_nt(ar[q_], bk[q_]) for q_ in groups]
        yield
        top = [jnp.where(lower_strict2, m[q_][0:rg_n, :], 0.0) for q_ in groups]
        bot = [jnp.where(lower_incl2, m[q_][rg_n:2 * rg_n, :], 0.0) for q_ in groups]
        p = [top[q_][:, 0:rg_n] for q_ in groups]
        akv = [_dot(top[q_][:, rg_n:2 * rg_n], vg[q_]) for q_ in groups]
        yield
        x = [jnp.concatenate([ar[q_][0:rg_n, :], akv[q_]], axis=1) for q_ in groups]
        for _ in range(5):
            px = [_dot(p[q_], jnp.concatenate([x[q_], p[q_]], axis=1)) for q_ in groups]
            x = [x[q_] + px[q_][:, 0:2 * HEAD_DIM] for q_ in groups]
            p = [px[q_][:, 2 * HEAD_DIM:2 * HEAD_DIM + rg_n] for q_ in groups]
            yield
        x = [x[q_] + _dot(p[q_], x[q_]) for q_ in groups]
        yield
        s_st = s_scr[h]
        for q_ in groups:
            us, rss = [], []
            for j in range(GROUP):
                c = q_ * GROUP + j
                rs = _rows(c)
                xj = x[q_][_rows(j), :]
                wr = jnp.concatenate([xj[:, 0:HEAD_DIM], r_pos[rs, sl]], axis=0)
                gs = _dot_nt(wr, s_st)
                yield
                u = gs[0:CHUNK, :] + xj[:, HEAD_DIM:2 * HEAD_DIM]
                uv = jnp.concatenate([u, v[rs, sl]], axis=0)
                bk_end = jnp.concatenate([b_end[rs, sl], k_end[rs, sl]], axis=0)
                s_st = s_st * e_pos[(c + 1) * CHUNK - 1:(c + 1) * CHUNK, sl] + _dot_tn(uv, bk_end)
                us.append(u)
                rss.append(gs[CHUNK:2 * CHUNK, :])
                yield
            uvg = jnp.concatenate(us + [vg[q_]], axis=0)
            pre_ref[_grows(q_), sl] = jnp.concatenate(rss, axis=0) + _dot(bot[q_], uvg)
        s_scr[h] = s_st

    return [head(h) for h in range(W_HEADS)], (g, bonus_v)


def _rwkv_finish(pre, g, bonus_v, ln_g, ln_b, bd_bf):
    return (_heads_ln(pre, RWKV_LN_EPS, bd_bf) * ln_g + ln_b + bonus_v) * g


def _run_interleaved(gens):
    pending = list(gens)
    while pending:
        for g in list(pending):
            try:
                next(g)
            except StopIteration:
                pending.remove(g)


def _mix_kernel(pm_ref, pr_ref, pw_ref, gt_ref, cos_ref, sin_ref, qdec_ref, kdec_ref, dmat_ref,
                cdec_ref, conv_ref, gbias_ref, bi_col_ref, bf_col_ref, mln_ref,
                mu_ref, w0_ref, wup_ref, a0_ref, aup_ref, gup_ref, kk_ref, ka_ref, rk_ref,
                wlng_ref, wlnb_ref,
                y_ref,
                cn_scr, m_scr, r_scr, s_scr, prevqk_scr, prevpw_scr, prem_scr, prer_scr, prew_scr,
                *, nch):
    @pl.when(pl.program_id(1) == 0)
    def _():
        cn_scr[...] = jnp.zeros(cn_scr.shape, F32)
        m_scr[...] = jnp.full(m_scr.shape, M_INIT, F32)
        r_scr[...] = jnp.zeros(r_scr.shape, F32)
        s_scr[...] = jnp.zeros(s_scr.shape, F32)
        prevqk_scr[...] = jnp.zeros(prevqk_scr.shape, F32)
        prevpw_scr[...] = jnp.zeros(prevpw_scr.shape, F32)

    rg_n = GROUP * CHUNK
    row = lax.broadcasted_iota(jnp.int32, (CHUNK, CHUNK), 0)
    col = lax.broadcasted_iota(jnp.int32, (CHUNK, CHUNK), 1)
    tril_bf = jnp.where(col <= row, 1.0, 0.0).astype(BF16)
    rowb = lax.broadcasted_iota(jnp.int32, (BD_W, BD_W), 0)
    colb = lax.broadcasted_iota(jnp.int32, (BD_W, BD_W), 1)
    bd_bf = jnp.where((rowb >> 6) == (colb >> 6), 1.0, 0.0).astype(BF16)
    row2 = lax.broadcasted_iota(jnp.int32, (rg_n, 2 * rg_n), 0)
    col2 = lax.broadcasted_iota(jnp.int32, (rg_n, 2 * rg_n), 1) & (rg_n - 1)
    start2 = (row2 >> 6) << 6
    lower_incl2 = (col2 <= row2) & (col2 >= start2)
    lower_strict2 = (col2 < row2) & (col2 >= start2)
    causal_g = lower_incl2[:, 0:rg_n]
    rowg = lax.broadcasted_iota(jnp.int32, (rg_n, rg_n), 0)
    colg = lax.broadcasted_iota(jnp.int32, (rg_n, rg_n), 1)
    triu_g_bf = jnp.where((rowg <= colg) & (rowg >= ((colg >> 6) << 6)), 1.0, 0.0).astype(BF16)

    tt = nch * CHUNK
    pw = pw_ref[0]
    w_gens, (w_gate, bonus_v) = _rwkv_heads(
        pw, prevpw_scr[...], mu_ref[...], w0_ref[...], wup_ref[...], a0_ref[...], aup_ref[...],
        gup_ref[...], kk_ref[...], ka_ref[...], rk_ref[...], s_scr, prew_scr, tril_bf, bd_bf,
        lower_incl2, lower_strict2, nch)
    prevpw_scr[...] = pw[tt - TAIL:tt, :]
    pm = pm_ref[0]
    m_gens, qk_raw = _mlstm_heads(pm, gt_ref[...], prevqk_scr[...], conv_ref[...], gbias_ref[...],
                                  bi_col_ref[...], bf_col_ref[...], cn_scr, m_scr, prem_scr,
                                  tril_bf, triu_g_bf, causal_g, nch)
    prevqk_scr[...] = qk_raw[tt - TAIL:tt, :]
    pr = pr_ref[0]
    r_gens = _retention_heads(pr, cos_ref[...], sin_ref[...], dmat_ref, qdec_ref[...],
                              kdec_ref[...], cdec_ref[...], r_scr, prer_scr, nch)
    _run_interleaved(w_gens + m_gens + r_gens)

    y_ref[0, :, 0:M_WIDTH] = _mlstm_finish(pm, prem_scr[...], mln_ref[...], bd_bf).astype(y_ref.dtype)
    y_ref[0, :, M_WIDTH:M_WIDTH + R_WIDTH] = _retention_finish(pr, prer_scr[...], bd_bf).astype(y_ref.dtype)
    y_ref[0, :, M_WIDTH + R_WIDTH:M_WIDTH + R_WIDTH + W_WIDTH] = _rwkv_finish(
        prew_scr[...], w_gate, bonus_v, wlng_ref[...], wlnb_ref[...], bd_bf).astype(y_ref.dtype)


def _full(shape):
    nd = len(shape)
    return pl.BlockSpec(shape, lambda b, c: (0,) * nd)


def _mixers(pm, pr, pw, gt, tabs, layer_params, l, nch):
    bsz, t, _ = pm.shape
    tt = nch * CHUNK
    nblk = t // tt
    tab_args = list(tabs)
    args = [pm, pr, pw, gt] + tab_args + list(layer_params)
    in_specs = [
        pl.BlockSpec((1, tt, PM_W), lambda b, c: (b, c, 0)),
        pl.BlockSpec((1, tt, R_COLS), lambda b, c: (b, c, 0)),
        pl.BlockSpec((1, tt, W_COLS), lambda b, c: (b, c, 0)),
        pl.BlockSpec((GT_ROWS, tt), lambda b, c: (0, b * nblk + c)),
    ] + [pl.BlockSpec((tt, R_WIDTH), lambda b, c: (c, 0)) for _ in range(4)] + [
        _full(a.shape) for a in tab_args[4:]] + [_layer_spec(a, l, 2) for a in layer_params]
    d_out = M_WIDTH + R_WIDTH + W_WIDTH
    return pl.pallas_call(
        functools.partial(_mix_kernel, nch=nch),
        grid=(bsz, t // tt),
        in_specs=in_specs,
        out_specs=pl.BlockSpec((1, tt, d_out), lambda b, c: (b, c, 0)),
        out_shape=jax.ShapeDtypeStruct((bsz, t, d_out), BF16),
        scratch_shapes=[
            pltpu.VMEM((M_HEADS, HEAD_DIM, 2 * HEAD_DIM), F32),
            pltpu.VMEM((1, 128), F32),
            pltpu.VMEM((R_HEADS, HEAD_DIM, HEAD_DIM), F32),
            pltpu.VMEM((W_HEADS, HEAD_DIM, HEAD_DIM), F32),
            pltpu.VMEM((TAIL, 2 * M_WIDTH), F32),
            pltpu.VMEM((TAIL, W_COLS), F32),
            pltpu.VMEM((tt, M_WIDTH), F32),
            pltpu.VMEM((tt, R_WIDTH), F32),
            pltpu.VMEM((tt, W_WIDTH), F32),
        ],
        compiler_params=pltpu.CompilerParams(dimension_semantics=("arbitrary", "arbitrary"),
                                             vmem_limit_bytes=VMEM_LIMIT),
        name="mixers",
    )(*args)


def _ffn_kernel(y_ref, wo_ref, x_ref, g_ref, wg_ref, wu_ref, wd_ref, gf_ref, o_ref, *, ff_chunk, final_norm):
    x = x_ref[...] + jnp.dot(y_ref[...], wo_ref[...], preferred_element_type=F32)
    hb = _bf(_rms(x, g_ref[...]))
    acc = x
    d_ff = wg_ref.shape[1]
    for c in range(d_ff // ff_chunk):
        cs = slice(c * ff_chunk, (c + 1) * ff_chunk)
        gate = jnp.dot(hb, wg_ref[:, cs], preferred_element_type=F32)
        up = jnp.dot(hb, wu_ref[:, cs], preferred_element_type=F32)
        acc = acc + jnp.dot(_bf(_silu(gate) * up), wd_ref[cs, :], preferred_element_type=F32)
    if final_norm:
        acc = _rms(acc, gf_ref[...])
    o_ref[...] = acc


def _out_ffn(y2, x2, wo_all, g_all, wg_all, wu_all, wd_all, gfin, l, tm, final_norm):
    n, d = x2.shape
    kern = functools.partial(_ffn_kernel, ff_chunk=FF_CHUNK, final_norm=final_norm)
    return pl.pallas_call(
        kern,
        grid=(n // tm,),
        in_specs=[pl.BlockSpec((tm, y2.shape[1]), lambda i: (i, 0)),
                  _layer_spec(wo_all, l, 1),
                  pl.BlockSpec((tm, d), lambda i: (i, 0)),
                  _layer_spec(g_all, l, 1),
                  _layer_spec(wg_all, l, 1),
                  _layer_spec(wu_all, l, 1),
                  _layer_spec(wd_all, l, 1),
                  pl.BlockSpec((1, d), lambda i: (0, 0))],
        out_specs=pl.BlockSpec((tm, d), lambda i: (i, 0)),
        out_shape=jax.ShapeDtypeStruct((n, d), F32),
        compiler_params=pltpu.CompilerParams(dimension_semantics=("arbitrary",),
                                             vmem_limit_bytes=VMEM_LIMIT),
        name="out_ffn",
    )(y2, wo_all, x2, g_all, wg_all, wu_all, wd_all, gfin)


def _retention_tables(t):
    h, d = R_HEADS, HEAD_DIM
    pos = jnp.arange(t, dtype=F32)
    theta = 1.0 / (ROPE_BASE ** jnp.linspace(0.0, 1.0, d // 2, dtype=F32))
    ang = pos[:, None] * theta[None, :]
    cos, sin = jnp.cos(ang), jnp.sin(ang)
    cos_t = jnp.tile(jnp.concatenate([cos, cos], axis=1), (1, h))
    sin_t = jnp.tile(jnp.concatenate([-sin, sin], axis=1), (1, h))
    log_g = jnp.log(1.0 - 2.0 ** (-5.0 - jnp.arange(h, dtype=F32)))
    idx = jnp.arange(CHUNK, dtype=F32)
    diff = idx[:, None] - idx[None, :]
    dmat = jnp.where(diff >= 0, jnp.exp(jnp.maximum(diff, 0.0)[None] * log_g[:, None, None]), 0.0)
    q_dec = jnp.exp((idx[None, :] + 1.0) * log_g[:, None])
    k_dec = jnp.exp((CHUNK - 1.0 - idx[None, :]) * log_g[:, None])
    c_dec = jnp.exp(CHUNK * log_g)
    expand = lambda z: jnp.repeat(z.T, d, axis=1)
    eye_g = jnp.eye(GROUP, dtype=F32)
    dmat_g = jnp.einsum('ab,hij->haibj', eye_g, dmat).reshape(h, GROUP * CHUNK, GROUP * CHUNK)
    reps = t // CHUNK
    return (cos_t, sin_t, jnp.tile(expand(q_dec), (reps, 1)), jnp.tile(expand(k_dec), (reps, 1)),
            dmat_g, jnp.repeat(c_dec, d)[None, :])


def kernel(x, w_in, ln1_g, ln2_g, lnf_g, m_conv, m_b_i, m_b_f, m_ln_g, rw_mu, rw_w0, rw_w_up,
           rw_a0, rw_a_up, rw_g_up, rw_k_k, rw_k_a, rw_r_k, rw_ln_g, rw_ln_b, w_out, w_gate,
           w_up, w_down):
    bsz, t, d = x.shape
    depth = w_in.shape[0]
    n = bsz * t
    nc = t // CHUNK
    tm = min(512, n)
    nch = math.gcd(nc, NCH)
    tabs = _retention_tables(t)
    x2 = x.reshape(n, d)
    rows3 = lambda z: z.reshape(depth, 1, -1)
    cols3 = lambda z: z.reshape(depth, -1, 1)
    w_pad = jnp.concatenate(
        [w_in[:, :, :M_COLS], jnp.zeros((depth, d, PM_W - M_COLS), w_in.dtype), w_in[:, :, M_COLS:]],
        axis=2).astype(BF16)
    gate_bias = jnp.concatenate(
        [m_b_i, m_b_f, jnp.zeros((depth, 128 - 2 * M_HEADS), F32)], axis=1)[:, None, :]
    layer_params = (m_conv, gate_bias, cols3(m_b_i), cols3(m_b_f), rows3(m_ln_g), rows3(rw_mu),
                    rows3(rw_w0), rw_w_up, rows3(rw_a0), rw_a_up, rw_g_up, rows3(rw_k_k),
                    rows3(rw_k_a), rows3(rw_r_k), rows3(rw_ln_g), rows3(rw_ln_b))
    wo_bf, wg_bf, wu_bf, wd_bf = (w.astype(BF16) for w in (w_out, w_gate, w_up, w_down))
    ln1, ln2 = rows3(ln1_g), rows3(ln2_g)
    for l in range(depth):
        pm, pr, pw, gt = _in_proj(x2, ln1, w_pad, l, tm)
        y = _mixers(pm.reshape(bsz, t, PM_W), pr.reshape(bsz, t, R_COLS), pw.reshape(bsz, t, W_COLS),
                    gt, tabs, layer_params, l, nch)
        x2 = _out_ffn(y.reshape(n, d), x2, wo_bf, ln2, wg_bf, wu_bf, wd_bf, lnf_g.reshape(1, -1), l, tm,
                      final_norm=(l == depth - 1))
    return x2.reshape(bsz, t, d)
```

```python
import functools
import math

import jax
import jax.numpy as jnp
from jax import lax
from jax.experimental import pallas as pl
from jax.experimental.pallas import tpu as pltpu

F32 = jnp.float32
BF16 = jnp.bfloat16

CHUNK = 64
HEAD_DIM = 64
M_HEADS, R_HEADS, W_HEADS = 4, 6, 6
M_WIDTH, R_WIDTH, W_WIDTH = 256, 384, 384
W_LORA, A_LORA, G_LORA = 64, 64, 128
M_COLS = 4 * M_WIDTH + 2 * M_HEADS
R_COLS = 4 * R_WIDTH
W_COLS = 3 * W_WIDTH + W_LORA + A_LORA + G_LORA
PM_W = 4 * M_WIDTH + 128
P_W = PM_W + R_COLS + W_COLS
RMS_EPS = 1e-6
HEAD_LN_EPS = 1e-5
RWKV_LN_EPS = 64e-5
ROPE_BASE = 10000.0
M_INIT = -1e30
VMEM_LIMIT = 56 * 1024 * 1024
TAIL = 8
NCH = 8
GROUP = 2
BD_W = 256
FF_CHUNK = 256
GT_ROWS = 8


def _bf(x):
    return x.astype(BF16)


def _dot(a, b):
    return jnp.dot(_bf(a), _bf(b), preferred_element_type=F32)


def _dot_nt(a, b):
    return lax.dot_general(_bf(a), _bf(b), (((1,), (1,)), ((), ())), preferred_element_type=F32)


def _dot_tn(a, b):
    return lax.dot_general(_bf(a), _bf(b), (((0,), (0,)), ((), ())), preferred_element_type=F32)


def _split2(x):
    hi = _bf(x)
    lo = _bf(x - hi.astype(F32))
    return hi, lo


def _split3(x):
    x1 = _bf(x)
    r = x - x1.astype(F32)
    x2 = _bf(r)
    x3 = _bf(r - x2.astype(F32))
    return x1, x2, x3


def _dot_exact_rhs(m_bf, x):
    x1, x2, x3 = _split3(x)
    return (jnp.dot(m_bf, x1, preferred_element_type=F32)
            + jnp.dot(m_bf, x2, preferred_element_type=F32)
            + jnp.dot(m_bf, x3, preferred_element_type=F32))


def _dot_exact_lhs(x, m_bf):
    x1, x2, x3 = _split3(x)
    return (jnp.dot(x1, m_bf, preferred_element_type=F32)
            + jnp.dot(x2, m_bf, preferred_element_type=F32)
            + jnp.dot(x3, m_bf, preferred_element_type=F32))


def _sigmoid(x):
    return 1.0 / (1.0 + jnp.exp(-x))


def _silu(x):
    return x * _sigmoid(x)


def _softplus(x):
    return jnp.maximum(x, 0.0) + jnp.log(1.0 + jnp.exp(-jnp.abs(x)))


def _log_sigmoid(x):
    return -_softplus(-x)


def _rms(x, g):
    return x * lax.rsqrt(jnp.mean(x * x, axis=-1, keepdims=True) + RMS_EPS) * g


def _rot_half(x):
    pieces = []
    for i in range(x.shape[1] // 128):
        p = x[:, 128 * i:128 * (i + 1)]
        lane = lax.broadcasted_iota(jnp.int32, p.shape, 1)
        pieces.append(jnp.where((lane & 63) < 32, pltpu.roll(p, 96, 1), pltpu.roll(p, 32, 1)))
    return jnp.concatenate(pieces, axis=1)


def _in_kernel(x_ref, g_ref, w_ref, pm_ref, pr_ref, pw_ref, gt_ref):
    hb = _bf(_rms(x_ref[...], g_ref[...]))
    p = jnp.dot(hb, w_ref[...], preferred_element_type=F32)
    pm_ref[...] = p[:, 0:PM_W]
    pr_ref[...] = p[:, PM_W:PM_W + R_COLS]
    pw_ref[...] = p[:, PM_W + R_COLS:P_W]
    gt_ref[...] = jnp.transpose(p[:, 4 * M_WIDTH:PM_W])[0:GT_ROWS, :]


def _layer_spec(arr, l, grid_rank):
    zeros = (0,) * (arr.ndim - 1)
    if grid_rank == 1:
        return pl.BlockSpec((None,) + arr.shape[1:], lambda i: (l,) + zeros)
    return pl.BlockSpec((None,) + arr.shape[1:], lambda b, c: (l,) + zeros)


def _in_proj(x2, g_all, w_pad_all, l, tm):
    n, d = x2.shape
    return pl.pallas_call(
        _in_kernel,
        grid=(n // tm,),
        in_specs=[pl.BlockSpec((tm, d), lambda i: (i, 0)),
                  _layer_spec(g_all, l, 1),
                  _layer_spec(w_pad_all, l, 1)],
        out_specs=[pl.BlockSpec((tm, PM_W), lambda i: (i, 0)),
                   pl.BlockSpec((tm, R_COLS), lambda i: (i, 0)),
                   pl.BlockSpec((tm, W_COLS), lambda i: (i, 0)),
                   pl.BlockSpec((GT_ROWS, tm), lambda i: (0, i))],
        out_shape=[jax.ShapeDtypeStruct((n, PM_W), F32),
                   jax.ShapeDtypeStruct((n, R_COLS), F32),
                   jax.ShapeDtypeStruct((n, W_COLS), F32),
                   jax.ShapeDtypeStruct((GT_ROWS, n), F32)],
        compiler_params=pltpu.CompilerParams(dimension_semantics=("arbitrary",),
                                             vmem_limit_bytes=VMEM_LIMIT),
        name="in_proj",
    )(x2, g_all, w_pad_all)


def _rows(c):
    return slice(c * CHUNK, (c + 1) * CHUNK)


def _grows(g):
    return slice(g * GROUP * CHUNK, (g + 1) * GROUP * CHUNK)


def _lanes(h):
    return slice(h * HEAD_DIM, (h + 1) * HEAD_DIM)


def _shift_rows(cur, prev_tail, j):
    rolled = pltpu.roll(cur, j, 0)
    row = lax.broadcasted_iota(jnp.int32, prev_tail.shape, 0)
    head = jnp.where(row >= j, rolled[0:TAIL, :], pltpu.roll(prev_tail, j, 0))
    return jnp.concatenate([head, rolled[TAIL:, :]], axis=0)


def _cummax_rows(x):
    row = lax.broadcasted_iota(jnp.int32, x.shape, 0)
    step = 1
    while step < x.shape[0]:
        x = jnp.maximum(x, jnp.where(row >= step, pltpu.roll(x, step, 0), -jnp.inf))
        step *= 2
    return x


def _head_sums(x, bd_bf, signed):
    outs = []
    off = 0
    while off < x.shape[1]:
        n = min(BD_W, x.shape[1] - off)
        b = bd_bf[0:n, 0:n]
        if signed:
            hi, lo = _split2(x[:, off:off + n])
            outs.append(jnp.dot(hi, b, preferred_element_type=F32) + jnp.dot(lo, b, preferred_element_type=F32))
        else:
            outs.append(jnp.dot(_bf(x[:, off:off + n]), b, preferred_element_type=F32))
        off += n
    return jnp.concatenate(outs, axis=1)


def _heads_ln(x, eps, bd_bf):
    mu = _head_sums(x, bd_bf, True) * (1.0 / HEAD_DIM)
    xc = x - mu
    var = _head_sums(xc * xc, bd_bf, False) * (1.0 / HEAD_DIM)
    return xc * lax.rsqrt(var + eps)


def _mlstm_heads(pm, gt, prev_tail, conv_w, gate_bias, b_i_col, b_f_col, cn_scr, m_scr, pre_ref,
                 tril_bf, triu_g_bf, causal_g, nch):
    chunks = range(nch)
    groups = range(nch // GROUP)
    qk_raw = pm[:, 0:2 * M_WIDTH]
    acc = qk_raw * conv_w[3:4, :]
    for j in (1, 2, 3):
        acc = acc + _shift_rows(qk_raw, prev_tail, j) * conv_w[3 - j:4 - j, :]
    qk = _silu(acc)
    q = qk[:, 0:M_WIDTH]
    k = qk[:, M_WIDTH:2 * M_WIDTH] * (HEAD_DIM ** -0.5)
    v = pm[:, 2 * M_WIDTH:3 * M_WIDTH]
    g128 = pm[:, 4 * M_WIDTH:4 * M_WIDTH + 128] + gate_bias
    lf128 = _log_sigmoid(g128)
    m = m_scr[...]
    b_al, cm, m_old, m_new, b_last = [], [], [], [], []
    for c in chunks:
        b_c = pltpu.roll(_dot_exact_rhs(tril_bf, lf128[_rows(c), :]), 128 - M_HEADS, 1)
        cm_c = _cummax_rows(g128[_rows(c), :] - b_c)
        bl = b_c[CHUNK - 1:CHUNK, :]
        b_al.append(b_c)
        cm.append(cm_c)
        b_last.append(bl)
        m_old.append(m)
        m = bl + jnp.maximum(m, cm_c[CHUNK - 1:CHUNK, :])
        m_new.append(m)
    m_scr[...] = m
    cat = lambda xs: jnp.concatenate(xs, axis=0)
    top = [jnp.maximum(m_old[c], cm[c]) for c in chunks]
    neg_top = cat([-t for t in top])
    inter_w = cat([jnp.exp(m_old[c] - top[c]) for c in chunks])
    e_mrow = cat([jnp.exp(-(b_al[c] + top[c])) for c in chunks])
    sw = cat([jnp.exp(b_last[c] - b_al[c] + g128[_rows(c), :] - m_new[c]) for c in chunks])
    dec = [jnp.exp(b_last[c] + m_old[c] - m_new[c]) for c in chunks]
    d_row = []
    for g in groups:
        i_r = gt[0:M_HEADS, _grows(g)] + b_i_col
        lf_r = _log_sigmoid(gt[M_HEADS:2 * M_HEADS, _grows(g)] + b_f_col)
        d_row.append(i_r - _dot_exact_lhs(lf_r, triu_g_bf))
    ones = jnp.ones((GROUP * CHUNK, HEAD_DIM), F32)

    def head(h):
        sl = _lanes(h)
        hc = slice(h, h + 1)
        qk_t = [_dot_nt(q[_grows(g), sl], k[_grows(g), sl]) for g in groups]
        vaug = [jnp.concatenate([v[_grows(g), sl], ones], axis=1) for g in groups]
        cn_in = [_dot_tn(k[_rows(c), sl] * sw[_rows(c), hc],
                         vaug[c // GROUP][_rows(c % GROUP), :]) for c in chunks]
        yield
        e = [jnp.exp(jnp.where(causal_g, neg_top[_grows(g), hc] + d_row[g][hc, :], -jnp.inf))
             for g in groups]
        sva = [_dot(qk_t[g] * e[g], vaug[g]) for g in groups]
        cn = cn_scr[h]
        qcn = []
        for c in chunks:
            qcn.append(_dot(q[_rows(c), sl], cn))
            cn = dec[c][:, hc] * cn + cn_in[c]
        cn_scr[h] = cn
        yield
        for g in groups:
            nd = sva[g] + inter_w[_grows(g), hc] * jnp.concatenate(qcn[g * GROUP:(g + 1) * GROUP], axis=0)
            pre_ref[_grows(g), sl] = nd[:, 0:HEAD_DIM] / jnp.maximum(
                jnp.abs(nd[:, HEAD_DIM:2 * HEAD_DIM]), e_mrow[_grows(g), hc])

    return [head(h) for h in range(M_HEADS)], qk_raw


def _mlstm_finish(pm, pre, ln_g, bd_bf):
    o = pm[:, 3 * M_WIDTH:4 * M_WIDTH]
    return _sigmoid(o) * (_heads_ln(pre, HEAD_LN_EPS, bd_bf) * ln_g)


def _retention_heads(pr, cos, sin, dmat_ref, qdec, kdec, cdec, r_scr, pre_ref, nch):
    chunks = range(nch)
    groups = range(nch // GROUP)
    q = pr[:, 0:R_WIDTH]
    k = pr[:, R_WIDTH:2 * R_WIDTH]
    v = pr[:, 2 * R_WIDTH:3 * R_WIDTH]
    qr = q * cos + _rot_half(q) * sin
    kr = (k * cos + _rot_half(k) * sin) * (HEAD_DIM ** -0.5)
    qd = qr * qdec
    kd = kr * kdec

    def head(h):
        sl = _lanes(h)
        qk_t = [_dot_nt(qr[_grows(g), sl], kr[_grows(g), sl]) for g in groups]
        r_in = [_dot_tn(kd[_rows(c), sl], v[_rows(c), sl]) for c in chunks]
        yield
        intra = [_dot(qk_t[g] * dmat_ref[h], v[_grows(g), sl]) for g in groups]
        r_st = r_scr[h]
        cross = []
        for c in chunks:
            cross.append(_dot(qd[_rows(c), sl], r_st))
            r_st = cdec[:, sl] * r_st + r_in[c]
        r_scr[h] = r_st
        yield
        for g in groups:
            pre_ref[_grows(g), sl] = intra[g] + jnp.concatenate(cross[g * GROUP:(g + 1) * GROUP], axis=0)

    return [head(h) for h in range(R_HEADS)]


def _retention_finish(pr, pre, bd_bf):
    return _silu(pr[:, 3 * R_WIDTH:4 * R_WIDTH]) * _heads_ln(pre, HEAD_LN_EPS, bd_bf)


def _rwkv_heads(pw, prev_tail, mu, w0, w_up, a0, a_up, g_up, k_k, k_a, r_k,
                s_scr, pre_ref, tril_bf, bd_bf, lower_incl2, lower_strict2, nch):
    chunks = range(nch)
    groups = range(nch // GROUP)
    rg_n = GROUP * CHUNK
    ps = pw + mu * (_shift_rows(pw, prev_tail, 1) - pw)
    r = ps[:, 0:W_WIDTH]
    k = ps[:, W_WIDTH:2 * W_WIDTH]
    v = ps[:, 2 * W_WIDTH:3 * W_WIDTH]
    o1 = 3 * W_WIDTH
    wl = ps[:, o1:o1 + W_LORA]
    al = ps[:, o1 + W_LORA:o1 + W_LORA + A_LORA]
    gl = ps[:, o1 + W_LORA + A_LORA:o1 + W_LORA + A_LORA + G_LORA]
    lw = -math.exp(-0.5) * _sigmoid(w0 + _dot(jnp.tanh(wl), w_up))
    a = _sigmoid(a0 + _dot(al, a_up))
    g = _dot(_sigmoid(gl), g_up)
    kk0 = k * k_k
    kk = kk0 / jnp.maximum(jnp.sqrt(_head_sums(kk0 * kk0, bd_bf, False)), 1e-12)
    k2 = k * (1.0 + (a - 1.0) * k_a)
    cum_c = [_dot_exact_rhs(tril_bf, lw[_rows(c), :]) for c in chunks]
    cum = jnp.concatenate(cum_c, axis=0)
    to_end = jnp.concatenate([cum_c[c][CHUNK - 1:CHUNK, :] - cum_c[c] for c in chunks], axis=0)
    e_pos = jnp.exp(cum)
    e_neg = jnp.exp(-cum)
    e_last = jnp.exp(to_end)
    b_t = -(kk * a)
    a_prev = kk * jnp.exp(cum - lw)
    b_neg = b_t * e_neg
    b_end = b_t * e_last
    r_pos = r * e_pos
    k_neg = k2 * e_neg
    k_end = k2 * e_last
    bonus_v = _head_sums(r * k2 * r_k, bd_bf, True) * v

    def head(h):
        sl = _lanes(h)
        ar = [jnp.concatenate([a_prev[_grows(q_), sl], r_pos[_grows(q_), sl]], axis=0) for q_ in groups]
        bk = [jnp.concatenate([b_neg[_grows(q_), sl], k_neg[_grows(q_), sl]], axis=0) for q_ in groups]
        vg = [v[_grows(q_), sl] for q_ in groups]
        m = [_dot_nt(ar[q_], bk[q_]) for q_ in groups]
        yield
        top = [jnp.where(lower_strict2, m[q_][0:rg_n, :], 0.0) for q_ in groups]
        bot = [jnp.where(lower_incl2, m[q_][rg_n:2 * rg_n, :], 0.0) for q_ in groups]
        p = [top[q_][:, 0:rg_n] for q_ in groups]
        akv = [_dot(top[q_][:, rg_n:2 * rg_n], vg[q_]) for q_ in groups]
        yield
        x = [jnp.concatenate([ar[q_][0:rg_n, :], akv[q_]], axis=1) for q_ in groups]
        for _ in range(5):
            px = [_dot(p[q_], jnp.concatenate([x[q_], p[q_]], axis=1)) for q_ in groups]
            x = [x[q_] + px[q_][:, 0:2 * HEAD_DIM] for q_ in groups]
            p = [px[q_][:, 2 * HEAD_DIM:2 * HEAD_DIM + rg_n] for q_ in groups]
            yield
        x = [x[q_] + _dot(p[q_], x[q_]) for q_ in groups]
        yield
        zeros = jnp.zeros((CHUNK, HEAD_DIM), F32)
        eg = []
        for c in chunks:
            rs = _rows(c)
            lhs = jnp.concatenate([x[c // GROUP][_rows(c % GROUP), :],
                                   jnp.concatenate([zeros, v[rs, sl]], axis=1)], axis=0)
            bk_end = jnp.concatenate([b_end[rs, sl], k_end[rs, sl]], axis=0)
            eg.append(_dot_tn(lhs, bk_end))
        yield
        s_st = s_scr[h]
        for q_ in groups:
            us, rss = [], []
            for j in range(GROUP):
                c = q_ * GROUP + j
                rs = _rows(c)
                xj = x[q_][_rows(j), :]
                wr = jnp.concatenate([xj[:, 0:HEAD_DIM], r_pos[rs, sl]], axis=0)
                gs = _dot_nt(wr, s_st)
                se = _dot(s_st, eg[c][0:HEAD_DIM, :])
                yield
                s_st = s_st * e_pos[(c + 1) * CHUNK - 1:(c + 1) * CHUNK, sl] + se + eg[c][HEAD_DIM:2 * HEAD_DIM, :]
                us.append(gs[0:CHUNK, :] + xj[:, HEAD_DIM:2 * HEAD_DIM])
                rss.append(gs[CHUNK:2 * CHUNK, :])
            uvg = jnp.concatenate(us + [vg[q_]], axis=0)
            pre_ref[_grows(q_), sl] = jnp.concatenate(rss, axis=0) + _dot(bot[q_], uvg)
        s_scr[h] = s_st

    return [head(h) for h in range(W_HEADS)], (g, bonus_v)


def _rwkv_finish(pre, g, bonus_v, ln_g, ln_b, bd_bf):
    return (_heads_ln(pre, RWKV_LN_EPS, bd_bf) * ln_g + ln_b + bonus_v) * g


def _run_interleaved(gens):
    pending = list(gens)
    while pending:
        for g in list(pending):
            try:
                next(g)
            except StopIteration:
                pending.remove(g)


def _mix_kernel(pm_ref, pr_ref, pw_ref, gt_ref, cos_ref, sin_ref, qdec_ref, kdec_ref, dmat_ref,
                cdec_ref, conv_ref, gbias_ref, bi_col_ref, bf_col_ref, mln_ref,
                mu_ref, w0_ref, wup_ref, a0_ref, aup_ref, gup_ref, kk_ref, ka_ref, rk_ref,
                wlng_ref, wlnb_ref,
                y_ref,
                cn_scr, m_scr, r_scr, s_scr, prevqk_scr, prevpw_scr, prem_scr, prer_scr, prew_scr,
                *, nch):
    @pl.when(pl.program_id(1) == 0)
    def _():
        cn_scr[...] = jnp.zeros(cn_scr.shape, F32)
        m_scr[...] = jnp.full(m_scr.shape, M_INIT, F32)
        r_scr[...] = jnp.zeros(r_scr.shape, F32)
        s_scr[...] = jnp.zeros(s_scr.shape, F32)
        prevqk_scr[...] = jnp.zeros(prevqk_scr.shape, F32)
        prevpw_scr[...] = jnp.zeros(prevpw_scr.shape, F32)

    rg_n = GROUP * CHUNK
    row = lax.broadcasted_iota(jnp.int32, (CHUNK, CHUNK), 0)
    col = lax.broadcasted_iota(jnp.int32, (CHUNK, CHUNK), 1)
    tril_bf = jnp.where(col <= row, 1.0, 0.0).astype(BF16)
    rowb = lax.broadcasted_iota(jnp.int32, (BD_W, BD_W), 0)
    colb = lax.broadcasted_iota(jnp.int32, (BD_W, BD_W), 1)
    bd_bf = jnp.where((rowb >> 6) == (colb >> 6), 1.0, 0.0).astype(BF16)
    row2 = lax.broadcasted_iota(jnp.int32, (rg_n, 2 * rg_n), 0)
    col2 = lax.broadcasted_iota(jnp.int32, (rg_n, 2 * rg_n), 1) & (rg_n - 1)
    start2 = (row2 >> 6) << 6
    lower_incl2 = (col2 <= row2) & (col2 >= start2)
    lower_strict2 = (col2 < row2) & (col2 >= start2)
    causal_g = lower_incl2[:, 0:rg_n]
    rowg = lax.broadcasted_iota(jnp.int32, (rg_n, rg_n), 0)
    colg = lax.broadcasted_iota(jnp.int32, (rg_n, rg_n), 1)
    triu_g_bf = jnp.where((rowg <= colg) & (rowg >= ((colg >> 6) << 6)), 1.0, 0.0).astype(BF16)

    tt = nch * CHUNK
    pw = pw_ref[0]
    w_gens, (w_gate, bonus_v) = _rwkv_heads(
        pw, prevpw_scr[...], mu_ref[...], w0_ref[...], wup_ref[...], a0_ref[...], aup_ref[...],
        gup_ref[...], kk_ref[...], ka_ref[...], rk_ref[...], s_scr, prew_scr, tril_bf, bd_bf,
        lower_incl2, lower_strict2, nch)
    prevpw_scr[...] = pw[tt - TAIL:tt, :]
    pm = pm_ref[0]
    m_gens, qk_raw = _mlstm_heads(pm, gt_ref[...], prevqk_scr[...], conv_ref[...], gbias_ref[...],
                                  bi_col_ref[...], bf_col_ref[...], cn_scr, m_scr, prem_scr,
                                  tril_bf, triu_g_bf, causal_g, nch)
    prevqk_scr[...] = qk_raw[tt - TAIL:tt, :]
    pr = pr_ref[0]
    r_gens = _retention_heads(pr, cos_ref[...], sin_ref[...], dmat_ref, qdec_ref[...],
                              kdec_ref[...], cdec_ref[...], r_scr, prer_scr, nch)
    _run_interleaved(w_gens + m_gens + r_gens)

    y_ref[0, :, 0:M_WIDTH] = _mlstm_finish(pm, prem_scr[...], mln_ref[...], bd_bf).astype(y_ref.dtype)
    y_ref[0, :, M_WIDTH:M_WIDTH + R_WIDTH] = _retention_finish(pr, prer_scr[...], bd_bf).astype(y_ref.dtype)
    y_ref[0, :, M_WIDTH + R_WIDTH:M_WIDTH + R_WIDTH + W_WIDTH] = _rwkv_finish(
        prew_scr[...], w_gate, bonus_v, wlng_ref[...], wlnb_ref[...], bd_bf).astype(y_ref.dtype)


def _full(shape):
    nd = len(shape)
    return pl.BlockSpec(shape, lambda b, c: (0,) * nd)


def _mixers(pm, pr, pw, gt, tabs, layer_params, l, nch):
    bsz, t, _ = pm.shape
    tt = nch * CHUNK
    nblk = t // tt
    tab_args = list(tabs)
    args = [pm, pr, pw, gt] + tab_args + list(layer_params)
    in_specs = [
        pl.BlockSpec((1, tt, PM_W), lambda b, c: (b, c, 0)),
        pl.BlockSpec((1, tt, R_COLS), lambda b, c: (b, c, 0)),
        pl.BlockSpec((1, tt, W_COLS), lambda b, c: (b, c, 0)),
        pl.BlockSpec((GT_ROWS, tt), lambda b, c: (0, b * nblk + c)),
    ] + [pl.BlockSpec((tt, R_WIDTH), lambda b, c: (c, 0)) for _ in range(4)] + [
        _full(a.shape) for a in tab_args[4:]] + [_layer_spec(a, l, 2) for a in layer_params]
    d_out = M_WIDTH + R_WIDTH + W_WIDTH
    return pl.pallas_call(
        functools.partial(_mix_kernel, nch=nch),
        grid=(bsz, t // tt),
        in_specs=in_specs,
        out_specs=pl.BlockSpec((1, tt, d_out), lambda b, c: (b, c, 0)),
        out_shape=jax.ShapeDtypeStruct((bsz, t, d_out), BF16),
        scratch_shapes=[
            pltpu.VMEM((M_HEADS, HEAD_DIM, 2 * HEAD_DIM), F32),
            pltpu.VMEM((1, 128), F32),
            pltpu.VMEM((R_HEADS, HEAD_DIM, HEAD_DIM), F32),
            pltpu.VMEM((W_HEADS, HEAD_DIM, HEAD_DIM), F32),
            pltpu.VMEM((TAIL, 2 * M_WIDTH), F32),
            pltpu.VMEM((TAIL, W_COLS), F32),
            pltpu.VMEM((tt, M_WIDTH), F32),
            pltpu.VMEM((tt, R_WIDTH), F32),
            pltpu.VMEM((tt, W_WIDTH), F32),
        ],
        compiler_params=pltpu.CompilerParams(dimension_semantics=("arbitrary", "arbitrary"),
                                             vmem_limit_bytes=VMEM_LIMIT),
        name="mixers",
    )(*args)


def _ffn_kernel(y_ref, wo_ref, x_ref, g_ref, wg_ref, wu_ref, wd_ref, gf_ref, o_ref, *, ff_chunk, final_norm):
    x = x_ref[...] + jnp.dot(y_ref[...], wo_ref[...], preferred_element_type=F32)
    hb = _bf(_rms(x, g_ref[...]))
    acc = x
    d_ff = wg_ref.shape[1]
    for c in range(d_ff // ff_chunk):
        cs = slice(c * ff_chunk, (c + 1) * ff_chunk)
        gate = jnp.dot(hb, wg_ref[:, cs], preferred_element_type=F32)
        up = jnp.dot(hb, wu_ref[:, cs], preferred_element_type=F32)
        acc = acc + jnp.dot(_bf(_silu(gate) * up), wd_ref[cs, :], preferred_element_type=F32)
    if final_norm:
        acc = _rms(acc, gf_ref[...])
    o_ref[...] = acc


def _out_ffn(y2, x2, wo_all, g_all, wg_all, wu_all, wd_all, gfin, l, tm, final_norm):
    n, d = x2.shape
    kern = functools.partial(_ffn_kernel, ff_chunk=FF_CHUNK, final_norm=final_norm)
    return pl.pallas_call(
        kern,
        grid=(n // tm,),
        in_specs=[pl.BlockSpec((tm, y2.shape[1]), lambda i: (i, 0)),
                  _layer_spec(wo_all, l, 1),
                  pl.BlockSpec((tm, d), lambda i: (i, 0)),
                  _layer_spec(g_all, l, 1),
                  _layer_spec(wg_all, l, 1),
                  _layer_spec(wu_all, l, 1),
                  _layer_spec(wd_all, l, 1),
                  pl.BlockSpec((1, d), lambda i: (0, 0))],
        out_specs=pl.BlockSpec((tm, d), lambda i: (i, 0)),
        out_shape=jax.ShapeDtypeStruct((n, d), F32),
        compiler_params=pltpu.CompilerParams(dimension_semantics=("arbitrary",),
                                             vmem_limit_bytes=VMEM_LIMIT),
        name="out_ffn",
    )(y2, wo_all, x2, g_all, wg_all, wu_all, wd_all, gfin)


def _retention_tables(t):
    h, d = R_HEADS, HEAD_DIM
    pos = jnp.arange(t, dtype=F32)
    theta = 1.0 / (ROPE_BASE ** jnp.linspace(0.0, 1.0, d // 2, dtype=F32))
    ang = pos[:, None] * theta[None, :]
    cos, sin = jnp.cos(ang), jnp.sin(ang)
    cos_t = jnp.tile(jnp.concatenate([cos, cos], axis=1), (1, h))
    sin_t = jnp.tile(jnp.concatenate([-sin, sin], axis=1), (1, h))
    log_g = jnp.log(1.0 - 2.0 ** (-5.0 - jnp.arange(h, dtype=F32)))
    idx = jnp.arange(CHUNK, dtype=F32)
    diff = idx[:, None] - idx[None, :]
    dmat = jnp.where(diff >= 0, jnp.exp(jnp.maximum(diff, 0.0)[None] * log_g[:, None, None]), 0.0)
    q_dec = jnp.exp((idx[None, :] + 1.0) * log_g[:, None])
    k_dec = jnp.exp((CHUNK - 1.0 - idx[None, :]) * log_g[:, None])
    c_dec = jnp.exp(CHUNK * log_g)
    expand = lambda z: jnp.repeat(z.T, d, axis=1)
    eye_g = jnp.eye(GROUP, dtype=F32)
    dmat_g = jnp.einsum('ab,hij->haibj', eye_g, dmat).reshape(h, GROUP * CHUNK, GROUP * CHUNK)
    reps = t // CHUNK
    return (cos_t, sin_t, jnp.tile(expand(q_dec), (reps, 1)), jnp.tile(expand(k_dec), (reps, 1)),
            dmat_g, jnp.repeat(c_dec, d)[None, :])


def kernel(x, w_in, ln1_g, ln2_g, lnf_g, m_conv, m_b_i, m_b_f, m_ln_g, rw_mu, rw_w0, rw_w_up,
           rw_a0, rw_a_up, rw_g_up, rw_k_k, rw_k_a, rw_r_k, rw_ln_g, rw_ln_b, w_out, w_gate,
           w_up, w_down):
    bsz, t, d = x.shape
    depth = w_in.shape[0]
    n = bsz * t
    nc = t // CHUNK
    tm = min(512, n)
    nch = math.gcd(nc, NCH)
    tabs = _retention_tables(t)
    x2 = x.reshape(n, d)
    rows3 = lambda z: z.reshape(depth, 1, -1)
    cols3 = lambda z: z.reshape(depth, -1, 1)
    w_pad = jnp.concatenate(
        [w_in[:, :, :M_COLS], jnp.zeros((depth, d, PM_W - M_COLS), w_in.dtype), w_in[:, :, M_COLS:]],
        axis=2).astype(BF16)
    gate_bias = jnp.concatenate(
        [m_b_i, m_b_f, jnp.zeros((depth, 128 - 2 * M_HEADS), F32)], axis=1)[:, None, :]
    layer_params = (m_conv, gate_bias, cols3(m_b_i), cols3(m_b_f), rows3(m_ln_g), rows3(rw_mu),
                    rows3(rw_w0), rw_w_up, rows3(rw_a0), rw_a_up, rw_g_up, rows3(rw_k_k),
                    rows3(rw_k_a), rows3(rw_r_k), rows3(rw_ln_g), rows3(rw_ln_b))
    wo_bf, wg_bf, wu_bf, wd_bf = (w.astype(BF16) for w in (w_out, w_gate, w_up, w_down))
    ln1, ln2 = rows3(ln1_g), rows3(ln2_g)
    for l in range(depth):
        pm, pr, pw, gt = _in_proj(x2, ln1, w_pad, l, tm)
        y = _mixers(pm.reshape(bsz, t, PM_W), pr.reshape(bsz, t, R_COLS), pw.reshape(bsz, t, W_COLS),
                    gt, tabs, layer_params, l, nch)
        x2 = _out_ffn(y.reshape(n, d), x2, wo_bf, ln2, wg_bf, wu_bf, wd_bf, lnf_g.reshape(1, -1), l, tm,
                      final_norm=(l == depth - 1))
    return x2.reshape(bsz, t, d)
```

```python
import functools
import math

import jax
import jax.numpy as jnp
from jax import lax
from jax.experimental import pallas as pl
from jax.experimental.pallas import tpu as pltpu

F32 = jnp.float32
BF16 = jnp.bfloat16

CHUNK = 64
HEAD_DIM = 64
M_HEADS, R_HEADS, W_HEADS = 4, 6, 6
M_WIDTH, R_WIDTH, W_WIDTH = 256, 384, 384
W_LORA, A_LORA, G_LORA = 64, 64, 128
M_COLS = 4 * M_WIDTH + 2 * M_HEADS
R_COLS = 4 * R_WIDTH
W_COLS = 3 * W_WIDTH + W_LORA + A_LORA + G_LORA
PM_W = 4 * M_WIDTH + 128
P_W = PM_W + R_COLS + W_COLS
RMS_EPS = 1e-6
HEAD_LN_EPS = 1e-5
RWKV_LN_EPS = 64e-5
ROPE_BASE = 10000.0
M_INIT = -1e30
VMEM_LIMIT = 56 * 1024 * 1024
TAIL = 8
NCH = 8
GROUP = 2
BD_W = 256
FF_CHUNK = 256
PROJ_SPLIT = (1280, 2816)
GT_ROWS = 8


def _bf(x):
    return x.astype(BF16)


def _dot(a, b):
    return jnp.dot(_bf(a), _bf(b), preferred_element_type=F32)


def _dot_nt(a, b):
    return lax.dot_general(_bf(a), _bf(b), (((1,), (1,)), ((), ())), preferred_element_type=F32)


def _dot_tn(a, b):
    return lax.dot_general(_bf(a), _bf(b), (((0,), (0,)), ((), ())), preferred_element_type=F32)


def _split2(x):
    hi = _bf(x)
    lo = _bf(x - hi.astype(F32))
    return hi, lo


def _split3(x):
    x1 = _bf(x)
    r = x - x1.astype(F32)
    x2 = _bf(r)
    x3 = _bf(r - x2.astype(F32))
    return x1, x2, x3


def _dot_exact_rhs(m_bf, x):
    x1, x2, x3 = _split3(x)
    return (jnp.dot(m_bf, x1, preferred_element_type=F32)
            + jnp.dot(m_bf, x2, preferred_element_type=F32)
            + jnp.dot(m_bf, x3, preferred_element_type=F32))


def _dot_exact_lhs(x, m_bf):
    x1, x2, x3 = _split3(x)
    return (jnp.dot(x1, m_bf, preferred_element_type=F32)
            + jnp.dot(x2, m_bf, preferred_element_type=F32)
            + jnp.dot(x3, m_bf, preferred_element_type=F32))


def _sigmoid(x):
    return 1.0 / (1.0 + jnp.exp(-x))


def _silu(x):
    return x * _sigmoid(x)


def _softplus(x):
    return jnp.maximum(x, 0.0) + jnp.log(1.0 + jnp.exp(-jnp.abs(x)))


def _log_sigmoid(x):
    return -_softplus(-x)


def _rms(x, g):
    return x * lax.rsqrt(jnp.mean(x * x, axis=-1, keepdims=True) + RMS_EPS) * g


def _rot_half(x):
    pieces = []
    for i in range(x.shape[1] // 128):
        p = x[:, 128 * i:128 * (i + 1)]
        lane = lax.broadcasted_iota(jnp.int32, p.shape, 1)
        pieces.append(jnp.where((lane & 63) < 32, pltpu.roll(p, 96, 1), pltpu.roll(p, 32, 1)))
    return jnp.concatenate(pieces, axis=1)


def _shift_rows(cur, prev_tail, j):
    rolled = pltpu.roll(cur, j, 0)
    row = lax.broadcasted_iota(jnp.int32, prev_tail.shape, 0)
    head = jnp.where(row >= j, rolled[0:TAIL, :], pltpu.roll(prev_tail, j, 0))
    return jnp.concatenate([head, rolled[TAIL:, :]], axis=0)


def _in_kernel(x_ref, g_ref, w_ref, conv_ref, mu_ref, cos_ref, sin_ref,
               pm_ref, pr_ref, pw_ref, gt_ref, qk_tail, pw_tail, *, steps_per_seq):
    @pl.when(lax.rem(pl.program_id(0), steps_per_seq) == 0)
    def _():
        qk_tail[...] = jnp.zeros(qk_tail.shape, F32)
        pw_tail[...] = jnp.zeros(pw_tail.shape, F32)

    tm = x_ref.shape[0]
    hb = _bf(_rms(x_ref[...], g_ref[...]))
    c1, c2 = PROJ_SPLIT
    p_a = jnp.dot(hb, w_ref[:, 0:c1], preferred_element_type=F32)
    p_b = jnp.dot(hb, w_ref[:, c1:c2], preferred_element_type=F32)

    conv_w = conv_ref[...]
    qk_raw = p_a[:, 0:2 * M_WIDTH]
    acc = qk_raw * conv_w[3:4, :]
    for j in (1, 2, 3):
        acc = acc + _shift_rows(qk_raw, qk_tail[...], j) * conv_w[3 - j:4 - j, :]
    qk = _silu(acc)
    pm_ref[:, 0:M_WIDTH] = qk[:, 0:M_WIDTH]
    pm_ref[:, M_WIDTH:2 * M_WIDTH] = qk[:, M_WIDTH:2 * M_WIDTH] * (HEAD_DIM ** -0.5)
    pm_ref[:, 2 * M_WIDTH:PM_W] = p_a[:, 2 * M_WIDTH:PM_W]
    qk_tail[...] = qk_raw[tm - TAIL:tm, :]
    gt_ref[...] = jnp.transpose(p_a[:, 4 * M_WIDTH:PM_W])[0:GT_ROWS, :]

    p_c = jnp.dot(hb, w_ref[:, c2:P_W], preferred_element_type=F32)

    pr = jnp.concatenate([p_a[:, PM_W:c1], p_b[:, 0:PM_W + R_COLS - c1]], axis=1)
    cos, sin = cos_ref[...], sin_ref[...]
    q = pr[:, 0:R_WIDTH]
    k = pr[:, R_WIDTH:2 * R_WIDTH]
    pr_ref[:, 0:R_WIDTH] = q * cos + _rot_half(q) * sin
    pr_ref[:, R_WIDTH:2 * R_WIDTH] = (k * cos + _rot_half(k) * sin) * (HEAD_DIM ** -0.5)
    pr_ref[:, 2 * R_WIDTH:R_COLS] = pr[:, 2 * R_WIDTH:R_COLS]

    pw = jnp.concatenate([p_b[:, PM_W + R_COLS - c1:c2 - c1], p_c], axis=1)
    pw_ref[...] = pw + mu_ref[...] * (_shift_rows(pw, pw_tail[...], 1) - pw)
    pw_tail[...] = pw[tm - TAIL:tm, :]


def _layer_spec(arr, l, grid_rank):
    zeros = (0,) * (arr.ndim - 1)
    if grid_rank == 1:
        return pl.BlockSpec((None,) + arr.shape[1:], lambda i: (l,) + zeros)
    return pl.BlockSpec((None,) + arr.shape[1:], lambda b, c: (l,) + zeros)


def _in_proj(x2, g_all, w_pad_all, conv_all, mu_all, cos, sin, l, t, tm):
    n, d = x2.shape
    steps_per_seq = t // tm
    tab = pl.BlockSpec((tm, R_WIDTH), lambda i: (lax.rem(i, steps_per_seq), 0))
    return pl.pallas_call(
        functools.partial(_in_kernel, steps_per_seq=steps_per_seq),
        grid=(n // tm,),
        in_specs=[pl.BlockSpec((tm, d), lambda i: (i, 0)),
                  _layer_spec(g_all, l, 1),
                  _layer_spec(w_pad_all, l, 1),
                  _layer_spec(conv_all, l, 1),
                  _layer_spec(mu_all, l, 1),
                  tab, tab],
        out_specs=[pl.BlockSpec((tm, PM_W), lambda i: (i, 0)),
                   pl.BlockSpec((tm, R_COLS), lambda i: (i, 0)),
                   pl.BlockSpec((tm, W_COLS), lambda i: (i, 0)),
                   pl.BlockSpec((GT_ROWS, tm), lambda i: (0, i))],
        out_shape=[jax.ShapeDtypeStruct((n, PM_W), F32),
                   jax.ShapeDtypeStruct((n, R_COLS), F32),
                   jax.ShapeDtypeStruct((n, W_COLS), F32),
                   jax.ShapeDtypeStruct((GT_ROWS, n), F32)],
        scratch_shapes=[pltpu.VMEM((TAIL, 2 * M_WIDTH), F32),
                        pltpu.VMEM((TAIL, W_COLS), F32)],
        compiler_params=pltpu.CompilerParams(dimension_semantics=("arbitrary",),
                                             vmem_limit_bytes=VMEM_LIMIT),
        name="in_proj",
    )(x2, g_all, w_pad_all, conv_all, mu_all, cos, sin)


def _rows(c):
    return slice(c * CHUNK, (c + 1) * CHUNK)


def _grows(g):
    return slice(g * GROUP * CHUNK, (g + 1) * GROUP * CHUNK)


def _lanes(h):
    return slice(h * HEAD_DIM, (h + 1) * HEAD_DIM)


def _cummax_rows(x):
    row = lax.broadcasted_iota(jnp.int32, x.shape, 0)
    step = 1
    while step < x.shape[0]:
        x = jnp.maximum(x, jnp.where(row >= step, pltpu.roll(x, step, 0), -jnp.inf))
        step *= 2
    return x


def _head_sums(x, bd_bf, signed):
    outs = []
    off = 0
    while off < x.shape[1]:
        n = min(BD_W, x.shape[1] - off)
        b = bd_bf[0:n, 0:n]
        if signed:
            hi, lo = _split2(x[:, off:off + n])
            outs.append(jnp.dot(hi, b, preferred_element_type=F32) + jnp.dot(lo, b, preferred_element_type=F32))
        else:
            outs.append(jnp.dot(_bf(x[:, off:off + n]), b, preferred_element_type=F32))
        off += n
    return jnp.concatenate(outs, axis=1)


def _heads_ln(x, eps, bd_bf):
    mu = _head_sums(x, bd_bf, True) * (1.0 / HEAD_DIM)
    xc = x - mu
    var = _head_sums(xc * xc, bd_bf, False) * (1.0 / HEAD_DIM)
    return xc * lax.rsqrt(var + eps)


def _mlstm_heads(pm, gt, gate_bias, b_i_col, b_f_col, cn_scr, m_scr, pre_ref,
                 tril_bf, triu_g_bf, causal_g, nch):
    chunks = range(nch)
    groups = range(nch // GROUP)
    q = pm[:, 0:M_WIDTH]
    k = pm[:, M_WIDTH:2 * M_WIDTH]
    v = pm[:, 2 * M_WIDTH:3 * M_WIDTH]
    g128 = pm[:, 4 * M_WIDTH:4 * M_WIDTH + 128] + gate_bias
    lf128 = _log_sigmoid(g128)
    m = m_scr[...]
    b_al, cm, m_old, m_new, b_last = [], [], [], [], []
    for c in chunks:
        b_c = pltpu.roll(_dot_exact_rhs(tril_bf, lf128[_rows(c), :]), 128 - M_HEADS, 1)
        cm_c = _cummax_rows(g128[_rows(c), :] - b_c)
        bl = b_c[CHUNK - 1:CHUNK, :]
        b_al.append(b_c)
        cm.append(cm_c)
        b_last.append(bl)
        m_old.append(m)
        m = bl + jnp.maximum(m, cm_c[CHUNK - 1:CHUNK, :])
        m_new.append(m)
    m_scr[...] = m
    cat = lambda xs: jnp.concatenate(xs, axis=0)
    top = [jnp.maximum(m_old[c], cm[c]) for c in chunks]
    neg_top = cat([-t for t in top])
    inter_w = cat([jnp.exp(m_old[c] - top[c]) for c in chunks])
    e_mrow = cat([jnp.exp(-(b_al[c] + top[c])) for c in chunks])
    sw = cat([jnp.exp(b_last[c] - b_al[c] + g128[_rows(c), :] - m_new[c]) for c in chunks])
    dec = [jnp.exp(b_last[c] + m_old[c] - m_new[c]) for c in chunks]
    d_row = []
    for g in groups:
        i_r = gt[0:M_HEADS, _grows(g)] + b_i_col
        lf_r = _log_sigmoid(gt[M_HEADS:2 * M_HEADS, _grows(g)] + b_f_col)
        d_row.append(i_r - _dot_exact_lhs(lf_r, triu_g_bf))
    ones = jnp.ones((GROUP * CHUNK, HEAD_DIM), F32)

    def head(h):
        sl = _lanes(h)
        hc = slice(h, h + 1)
        qk_t = [_dot_nt(q[_grows(g), sl], k[_grows(g), sl]) for g in groups]
        vaug = [jnp.concatenate([v[_grows(g), sl], ones], axis=1) for g in groups]
        cn_in = [_dot_tn(k[_rows(c), sl] * sw[_rows(c), hc],
                         vaug[c // GROUP][_rows(c % GROUP), :]) for c in chunks]
        yield
        e = [jnp.exp(jnp.where(causal_g, neg_top[_grows(g), hc] + d_row[g][hc, :], -jnp.inf))
             for g in groups]
        sva = [_dot(qk_t[g] * e[g], vaug[g]) for g in groups]
        cn = cn_scr[h]
        qcn = []
        for c in chunks:
            qcn.append(_dot(q[_rows(c), sl], cn))
            cn = dec[c][:, hc] * cn + cn_in[c]
        cn_scr[h] = cn
        yield
        for g in groups:
            nd = sva[g] + inter_w[_grows(g), hc] * jnp.concatenate(qcn[g * GROUP:(g + 1) * GROUP], axis=0)
            pre_ref[_grows(g), sl] = nd[:, 0:HEAD_DIM] / jnp.maximum(
                jnp.abs(nd[:, HEAD_DIM:2 * HEAD_DIM]), e_mrow[_grows(g), hc])

    return [head(h) for h in range(M_HEADS)]


def _mlstm_finish(pm, pre, ln_g, bd_bf):
    o = pm[:, 3 * M_WIDTH:4 * M_WIDTH]
    return _sigmoid(o) * (_heads_ln(pre, HEAD_LN_EPS, bd_bf) * ln_g)


def _retention_heads(pr, dmat_ref, qdec, kdec, cdec, r_scr, pre_ref, nch):
    chunks = range(nch)
    groups = range(nch // GROUP)
    qr = pr[:, 0:R_WIDTH]
    kr = pr[:, R_WIDTH:2 * R_WIDTH]
    v = pr[:, 2 * R_WIDTH:3 * R_WIDTH]
    qd = qr * qdec
    kd = kr * kdec

    def head(h):
        sl = _lanes(h)
        qk_t = [_dot_nt(qr[_grows(g), sl], kr[_grows(g), sl]) for g in groups]
        r_in = [_dot_tn(kd[_rows(c), sl], v[_rows(c), sl]) for c in chunks]
        yield
        intra = [_dot(qk_t[g] * dmat_ref[h], v[_grows(g), sl]) for g in groups]
        r_st = r_scr[h]
        cross = []
        for c in chunks:
            cross.append(_dot(qd[_rows(c), sl], r_st))
            r_st = cdec[:, sl] * r_st + r_in[c]
        r_scr[h] = r_st
        yield
        for g in groups:
            pre_ref[_grows(g), sl] = intra[g] + jnp.concatenate(cross[g * GROUP:(g + 1) * GROUP], axis=0)

    return [head(h) for h in range(R_HEADS)]


def _retention_finish(pr, pre, bd_bf):
    return _silu(pr[:, 3 * R_WIDTH:4 * R_WIDTH]) * _heads_ln(pre, HEAD_LN_EPS, bd_bf)


def _rwkv_heads(ps, w0, w_up, a0, a_up, g_up, k_k, k_a, r_k,
                s_scr, pre_ref, tril_bf, bd_bf, lower_incl2, lower_strict2, nch):
    chunks = range(nch)
    groups = range(nch // GROUP)
    rg_n = GROUP * CHUNK
    r = ps[:, 0:W_WIDTH]
    k = ps[:, W_WIDTH:2 * W_WIDTH]
    v = ps[:, 2 * W_WIDTH:3 * W_WIDTH]
    o1 = 3 * W_WIDTH
    wl = ps[:, o1:o1 + W_LORA]
    al = ps[:, o1 + W_LORA:o1 + W_LORA + A_LORA]
    gl = ps[:, o1 + W_LORA + A_LORA:o1 + W_LORA + A_LORA + G_LORA]
    lw = -math.exp(-0.5) * _sigmoid(w0 + _dot(jnp.tanh(wl), w_up))
    a = _sigmoid(a0 + _dot(al, a_up))
    g = _dot(_sigmoid(gl), g_up)
    kk0 = k * k_k
    kk = kk0 / jnp.maximum(jnp.sqrt(_head_sums(kk0 * kk0, bd_bf, False)), 1e-12)
    k2 = k * (1.0 + (a - 1.0) * k_a)
    cum_c = [_dot_exact_rhs(tril_bf, lw[_rows(c), :]) for c in chunks]
    cum = jnp.concatenate(cum_c, axis=0)
    to_end = jnp.concatenate([cum_c[c][CHUNK - 1:CHUNK, :] - cum_c[c] for c in chunks], axis=0)
    e_pos = jnp.exp(cum)
    e_neg = jnp.exp(-cum)
    e_last = jnp.exp(to_end)
    b_t = -(kk * a)
    a_prev = kk * jnp.exp(cum - lw)
    b_neg = b_t * e_neg
    b_end = b_t * e_last
    r_pos = r * e_pos
    k_neg = k2 * e_neg
    k_end = k2 * e_last
    bonus_v = _head_sums(r * k2 * r_k, bd_bf, True) * v

    def head(h):
        sl = _lanes(h)
        ar = [jnp.concatenate([a_prev[_grows(q_), sl], r_pos[_grows(q_), sl]], axis=0) for q_ in groups]
        bk = [jnp.concatenate([b_neg[_grows(q_), sl], k_neg[_grows(q_), sl]], axis=0) for q_ in groups]
        vg = [v[_grows(q_), sl] for q_ in groups]
        m = [_dot_nt(ar[q_], bk[q_]) for q_ in groups]
        yield
        top = [jnp.where(lower_strict2, m[q_][0:rg_n, :], 0.0) for q_ in groups]
        bot = [jnp.where(lower_incl2, m[q_][rg_n:2 * rg_n, :], 0.0) for q_ in groups]
        p = [top[q_][:, 0:rg_n] for q_ in groups]
        akv = [_dot(top[q_][:, rg_n:2 * rg_n], vg[q_]) for q_ in groups]
        yield
        x = [jnp.concatenate([ar[q_][0:rg_n, :], akv[q_]], axis=1) for q_ in groups]
        for _ in range(5):
            px = [_dot(p[q_], jnp.concatenate([x[q_], p[q_]], axis=1)) for q_ in groups]
            x = [x[q_] + px[q_][:, 0:2 * HEAD_DIM] for q_ in groups]
            p = [px[q_][:, 2 * HEAD_DIM:2 * HEAD_DIM + rg_n] for q_ in groups]
            yield
        x = [x[q_] + _dot(p[q_], x[q_]) for q_ in groups]
        yield
        zeros = jnp.zeros((CHUNK, HEAD_DIM), F32)
        eg = []
        for c in chunks:
            rs = _rows(c)
            lhs = jnp.concatenate([x[c // GROUP][_rows(c % GROUP), :],
                                   jnp.concatenate([zeros, v[rs, sl]], axis=1)], axis=0)
            bk_end = jnp.concatenate([b_end[rs, sl], k_end[rs, sl]], axis=0)
            eg.append(_dot_tn(lhs, bk_end))
        yield
        s_st = s_scr[h]
        for q_ in groups:
            us, rss = [], []
            for j in range(GROUP):
                c = q_ * GROUP + j
                rs = _rows(c)
                xj = x[q_][_rows(j), :]
                wr = jnp.concatenate([xj[:, 0:HEAD_DIM], r_pos[rs, sl]], axis=0)
                gs = _dot_nt(wr, s_st)
                se = _dot(s_st, eg[c][0:HEAD_DIM, :])
                yield
                s_st = s_st * e_pos[(c + 1) * CHUNK - 1:(c + 1) * CHUNK, sl] + se + eg[c][HEAD_DIM:2 * HEAD_DIM, :]
                us.append(gs[0:CHUNK, :] + xj[:, HEAD_DIM:2 * HEAD_DIM])
                rss.append(gs[CHUNK:2 * CHUNK, :])
            uvg = jnp.concatenate(us + [vg[q_]], axis=0)
            pre_ref[_grows(q_), sl] = jnp.concatenate(rss, axis=0) + _dot(bot[q_], uvg)
        s_scr[h] = s_st

    return [head(h) for h in range(W_HEADS)], (g, bonus_v)


def _rwkv_finish(pre, g, bonus_v, ln_g, ln_b, bd_bf):
    return (_heads_ln(pre, RWKV_LN_EPS, bd_bf) * ln_g + ln_b + bonus_v) * g


def _run_interleaved(gens):
    pending = list(gens)
    while pending:
        for g in list(pending):
            try:
                next(g)
            except StopIteration:
                pending.remove(g)


def _mix_kernel(pm_ref, pr_ref, pw_ref, gt_ref, qdec_ref, kdec_ref, dmat_ref,
                cdec_ref, gbias_ref, bi_col_ref, bf_col_ref, mln_ref,
                w0_ref, wup_ref, a0_ref, aup_ref, gup_ref, kk_ref, ka_ref, rk_ref,
                wlng_ref, wlnb_ref,
                y_ref,
                cn_scr, m_scr, r_scr, s_scr, prem_scr, prer_scr, prew_scr,
                *, nch):
    @pl.when(pl.program_id(1) == 0)
    def _():
        cn_scr[...] = jnp.zeros(cn_scr.shape, F32)
        m_scr[...] = jnp.full(m_scr.shape, M_INIT, F32)
        r_scr[...] = jnp.zeros(r_scr.shape, F32)
        s_scr[...] = jnp.zeros(s_scr.shape, F32)

    rg_n = GROUP * CHUNK
    row = lax.broadcasted_iota(jnp.int32, (CHUNK, CHUNK), 0)
    col = lax.broadcasted_iota(jnp.int32, (CHUNK, CHUNK), 1)
    tril_bf = jnp.where(col <= row, 1.0, 0.0).astype(BF16)
    rowb = lax.broadcasted_iota(jnp.int32, (BD_W, BD_W), 0)
    colb = lax.broadcasted_iota(jnp.int32, (BD_W, BD_W), 1)
    bd_bf = jnp.where((rowb >> 6) == (colb >> 6), 1.0, 0.0).astype(BF16)
    row2 = lax.broadcasted_iota(jnp.int32, (rg_n, 2 * rg_n), 0)
    col2 = lax.broadcasted_iota(jnp.int32, (rg_n, 2 * rg_n), 1) & (rg_n - 1)
    start2 = (row2 >> 6) << 6
    lower_incl2 = (col2 <= row2) & (col2 >= start2)
    lower_strict2 = (col2 < row2) & (col2 >= start2)
    causal_g = lower_incl2[:, 0:rg_n]
    rowg = lax.broadcasted_iota(jnp.int32, (rg_n, rg_n), 0)
    colg = lax.broadcasted_iota(jnp.int32, (rg_n, rg_n), 1)
    triu_g_bf = jnp.where((rowg <= colg) & (rowg >= ((colg >> 6) << 6)), 1.0, 0.0).astype(BF16)

    w_gens, (w_gate, bonus_v) = _rwkv_heads(
        pw_ref[0], w0_ref[...], wup_ref[...], a0_ref[...], aup_ref[...],
        gup_ref[...], kk_ref[...], ka_ref[...], rk_ref[...], s_scr, prew_scr, tril_bf, bd_bf,
        lower_incl2, lower_strict2, nch)
    pm = pm_ref[0]
    m_gens = _mlstm_heads(pm, gt_ref[...], gbias_ref[...], bi_col_ref[...], bf_col_ref[...],
                          cn_scr, m_scr, prem_scr, tril_bf, triu_g_bf, causal_g, nch)
    pr = pr_ref[0]
    r_gens = _retention_heads(pr, dmat_ref, qdec_ref[...], kdec_ref[...], cdec_ref[...],
                              r_scr, prer_scr, nch)
    _run_interleaved(w_gens + m_gens + r_gens)

    y_ref[0, :, 0:M_WIDTH] = _mlstm_finish(pm, prem_scr[...], mln_ref[...], bd_bf).astype(y_ref.dtype)
    y_ref[0, :, M_WIDTH:M_WIDTH + R_WIDTH] = _retention_finish(pr, prer_scr[...], bd_bf).astype(y_ref.dtype)
    y_ref[0, :, M_WIDTH + R_WIDTH:M_WIDTH + R_WIDTH + W_WIDTH] = _rwkv_finish(
        prew_scr[...], w_gate, bonus_v, wlng_ref[...], wlnb_ref[...], bd_bf).astype(y_ref.dtype)


def _full(shape):
    nd = len(shape)
    return pl.BlockSpec(shape, lambda b, c: (0,) * nd)


def _mixers(pm, pr, pw, gt, tabs, layer_params, l, nch):
    bsz, t, _ = pm.shape
    tt = nch * CHUNK
    nblk = t // tt
    tab_args = list(tabs)
    args = [pm, pr, pw, gt] + tab_args + list(layer_params)
    in_specs = [
        pl.BlockSpec((1, tt, PM_W), lambda b, c: (b, c, 0)),
        pl.BlockSpec((1, tt, R_COLS), lambda b, c: (b, c, 0)),
        pl.BlockSpec((1, tt, W_COLS), lambda b, c: (b, c, 0)),
        pl.BlockSpec((GT_ROWS, tt), lambda b, c: (0, b * nblk + c)),
    ] + [pl.BlockSpec((tt, R_WIDTH), lambda b, c: (c, 0)) for _ in range(2)] + [
        _full(a.shape) for a in tab_args[2:]] + [_layer_spec(a, l, 2) for a in layer_params]
    d_out = M_WIDTH + R_WIDTH + W_WIDTH
    return pl.pallas_call(
        functools.partial(_mix_kernel, nch=nch),
        grid=(bsz, t // tt),
        in_specs=in_specs,
        out_specs=pl.BlockSpec((1, tt, d_out), lambda b, c: (b, c, 0)),
        out_shape=jax.ShapeDtypeStruct((bsz, t, d_out), BF16),
        scratch_shapes=[
            pltpu.VMEM((M_HEADS, HEAD_DIM, 2 * HEAD_DIM), F32),
            pltpu.VMEM((1, 128), F32),
            pltpu.VMEM((R_HEADS, HEAD_DIM, HEAD_DIM), F32),
            pltpu.VMEM((W_HEADS, HEAD_DIM, HEAD_DIM), F32),
            pltpu.VMEM((tt, M_WIDTH), F32),
            pltpu.VMEM((tt, R_WIDTH), F32),
            pltpu.VMEM((tt, W_WIDTH), F32),
        ],
        compiler_params=pltpu.CompilerParams(dimension_semantics=("arbitrary", "arbitrary"),
                                             vmem_limit_bytes=VMEM_LIMIT),
        name="mixers",
    )(*args)


def _ffn_kernel(y_ref, wo_ref, x_ref, g_ref, wg_ref, wu_ref, wd_ref, gf_ref, o_ref, *, ff_chunk, final_norm):
    x = x_ref[...] + jnp.dot(y_ref[...], wo_ref[...], preferred_element_type=F32)
    hb = _bf(_rms(x, g_ref[...]))
    acc = x
    d_ff = wg_ref.shape[1]
    for c in range(d_ff // ff_chunk):
        cs = slice(c * ff_chunk, (c + 1) * ff_chunk)
        gate = jnp.dot(hb, wg_ref[:, cs], preferred_element_type=F32)
        up = jnp.dot(hb, wu_ref[:, cs], preferred_element_type=F32)
        acc = acc + jnp.dot(_bf(_silu(gate) * up), wd_ref[cs, :], preferred_element_type=F32)
    if final_norm:
        acc = _rms(acc, gf_ref[...])
    o_ref[...] = acc


def _out_ffn(y2, x2, wo_all, g_all, wg_all, wu_all, wd_all, gfin, l, tm, final_norm):
    n, d = x2.shape
    kern = functools.partial(_ffn_kernel, ff_chunk=FF_CHUNK, final_norm=final_norm)
    return pl.pallas_call(
        kern,
        grid=(n // tm,),
        in_specs=[pl.BlockSpec((tm, y2.shape[1]), lambda i: (i, 0)),
                  _layer_spec(wo_all, l, 1),
                  pl.BlockSpec((tm, d), lambda i: (i, 0)),
                  _layer_spec(g_all, l, 1),
                  _layer_spec(wg_all, l, 1),
                  _layer_spec(wu_all, l, 1),
                  _layer_spec(wd_all, l, 1),
                  pl.BlockSpec((1, d), lambda i: (0, 0))],
        out_specs=pl.BlockSpec((tm, d), lambda i: (i, 0)),
        out_shape=jax.ShapeDtypeStruct((n, d), F32),
        compiler_params=pltpu.CompilerParams(dimension_semantics=("arbitrary",),
                                             vmem_limit_bytes=VMEM_LIMIT),
        name="out_ffn",
    )(y2, wo_all, x2, g_all, wg_all, wu_all, wd_all, gfin)


def _retention_tables(t):
    h, d = R_HEADS, HEAD_DIM
    pos = jnp.arange(t, dtype=F32)
    theta = 1.0 / (ROPE_BASE ** jnp.linspace(0.0, 1.0, d // 2, dtype=F32))
    ang = pos[:, None] * theta[None, :]
    cos, sin = jnp.cos(ang), jnp.sin(ang)
    cos_t = jnp.tile(jnp.concatenate([cos, cos], axis=1), (1, h))
    sin_t = jnp.tile(jnp.concatenate([-sin, sin], axis=1), (1, h))
    log_g = jnp.log(1.0 - 2.0 ** (-5.0 - jnp.arange(h, dtype=F32)))
    idx = jnp.arange(CHUNK, dtype=F32)
    diff = idx[:, None] - idx[None, :]
    dmat = jnp.where(diff >= 0, jnp.exp(jnp.maximum(diff, 0.0)[None] * log_g[:, None, None]), 0.0)
    q_dec = jnp.exp((idx[None, :] + 1.0) * log_g[:, None])
    k_dec = jnp.exp((CHUNK - 1.0 - idx[None, :]) * log_g[:, None])
    c_dec = jnp.exp(CHUNK * log_g)
    expand = lambda z: jnp.repeat(z.T, d, axis=1)
    eye_g = jnp.eye(GROUP, dtype=F32)
    dmat_g = jnp.einsum('ab,hij->haibj', eye_g, dmat).reshape(h, GROUP * CHUNK, GROUP * CHUNK)
    reps = t // CHUNK
    return (cos_t, sin_t, jnp.tile(expand(q_dec), (reps, 1)), jnp.tile(expand(k_dec), (reps, 1)),
            dmat_g, jnp.repeat(c_dec, d)[None, :])


def kernel(x, w_in, ln1_g, ln2_g, lnf_g, m_conv, m_b_i, m_b_f, m_ln_g, rw_mu, rw_w0, rw_w_up,
           rw_a0, rw_a_up, rw_g_up, rw_k_k, rw_k_a, rw_r_k, rw_ln_g, rw_ln_b, w_out, w_gate,
           w_up, w_down):
    bsz, t, d = x.shape
    depth = w_in.shape[0]
    n = bsz * t
    nc = t // CHUNK
    tm = min(512, n)
    nch = math.gcd(nc, NCH)
    tabs = _retention_tables(t)
    x2 = x.reshape(n, d)
    rows3 = lambda z: z.reshape(depth, 1, -1)
    cols3 = lambda z: z.reshape(depth, -1, 1)
    w_pad = jnp.concatenate(
        [w_in[:, :, :M_COLS], jnp.zeros((depth, d, PM_W - M_COLS), w_in.dtype), w_in[:, :, M_COLS:]],
        axis=2).astype(BF16)
    gate_bias = jnp.concatenate(
        [m_b_i, m_b_f, jnp.zeros((depth, 128 - 2 * M_HEADS), F32)], axis=1)[:, None, :]
    layer_params = (gate_bias, cols3(m_b_i), cols3(m_b_f), rows3(m_ln_g),
                    rows3(rw_w0), rw_w_up, rows3(rw_a0), rw_a_up, rw_g_up, rows3(rw_k_k),
                    rows3(rw_k_a), rows3(rw_r_k), rows3(rw_ln_g), rows3(rw_ln_b))
    wo_bf, wg_bf, wu_bf, wd_bf = (w.astype(BF16) for w in (w_out, w_gate, w_up, w_down))
    ln1, ln2 = rows3(ln1_g), rows3(ln2_g)
    for l in range(depth):
        pm, pr, pw, gt = _in_proj(x2, ln1, w_pad, m_conv, rows3(rw_mu), tabs[0], tabs[1], l, t,
                                  math.gcd(t, tm))
        y = _mixers(pm.reshape(bsz, t, PM_W), pr.reshape(bsz, t, R_COLS), pw.reshape(bsz, t, W_COLS),
                    gt, tabs[2:], layer_params, l, nch)
        x2 = _out_ffn(y.reshape(n, d), x2, wo_bf, ln2, wg_bf, wu_bf, wd_bf, lnf_g.reshape(1, -1), l, tm,
                      final_norm=(l == depth - 1))
    return x2.reshape(bsz, t, d)
```

```python
import functools
import math

import jax
import jax.numpy as jnp
from jax import lax
from jax.experimental import pallas as pl
from jax.experimental.pallas import tpu as pltpu

F32 = jnp.float32
BF16 = jnp.bfloat16

CHUNK = 64
HEAD_DIM = 64
M_HEADS, R_HEADS, W_HEADS = 4, 6, 6
M_WIDTH, R_WIDTH, W_WIDTH = 256, 384, 384
W_LORA, A_LORA, G_LORA = 64, 64, 128
M_COLS = 4 * M_WIDTH + 2 * M_HEADS
R_COLS = 4 * R_WIDTH
W_COLS = 3 * W_WIDTH + W_LORA + A_LORA + G_LORA
PM_W = 4 * M_WIDTH + 128
P_W = PM_W + R_COLS + W_COLS
RMS_EPS = 1e-6
HEAD_LN_EPS = 1e-5
RWKV_LN_EPS = 64e-5
ROPE_BASE = 10000.0
M_INIT = -1e30
VMEM_LIMIT = 56 * 1024 * 1024
TAIL = 8
NCH = 8
GROUP = 2
BD_W = 256
FF_CHUNK = 256
PROJ_SPLIT = (1280, 2816)
GT_ROWS = 8


def _bf(x):
    return x.astype(BF16)


def _dot(a, b):
    return jnp.dot(_bf(a), _bf(b), preferred_element_type=F32)


def _dot_nt(a, b):
    return lax.dot_general(_bf(a), _bf(b), (((1,), (1,)), ((), ())), preferred_element_type=F32)


def _dot_tn(a, b):
    return lax.dot_general(_bf(a), _bf(b), (((0,), (0,)), ((), ())), preferred_element_type=F32)


def _split2(x):
    hi = _bf(x)
    lo = _bf(x - hi.astype(F32))
    return hi, lo


def _split3(x):
    x1 = _bf(x)
    r = x - x1.astype(F32)
    x2 = _bf(r)
    x3 = _bf(r - x2.astype(F32))
    return x1, x2, x3


def _dot_exact_rhs(m_bf, x):
    x1, x2, x3 = _split3(x)
    return (jnp.dot(m_bf, x1, preferred_element_type=F32)
            + jnp.dot(m_bf, x2, preferred_element_type=F32)
            + jnp.dot(m_bf, x3, preferred_element_type=F32))


def _dot_exact_lhs(x, m_bf):
    x1, x2, x3 = _split3(x)
    return (jnp.dot(x1, m_bf, preferred_element_type=F32)
            + jnp.dot(x2, m_bf, preferred_element_type=F32)
            + jnp.dot(x3, m_bf, preferred_element_type=F32))


def _sigmoid(x):
    return 1.0 / (1.0 + jnp.exp(-x))


def _silu(x):
    return x * _sigmoid(x)


def _softplus(x):
    return jnp.maximum(x, 0.0) + jnp.log(1.0 + jnp.exp(-jnp.abs(x)))


def _log_sigmoid(x):
    return -_softplus(-x)


def _rms(x, g):
    return x * lax.rsqrt(jnp.mean(x * x, axis=-1, keepdims=True) + RMS_EPS) * g


def _rot_half(x):
    pieces = []
    for i in range(x.shape[1] // 128):
        p = x[:, 128 * i:128 * (i + 1)]
        lane = lax.broadcasted_iota(jnp.int32, p.shape, 1)
        pieces.append(jnp.where((lane & 63) < 32, pltpu.roll(p, 96, 1), pltpu.roll(p, 32, 1)))
    return jnp.concatenate(pieces, axis=1)


def _shift_rows(cur, prev_tail, j):
    rolled = pltpu.roll(cur, j, 0)
    row = lax.broadcasted_iota(jnp.int32, prev_tail.shape, 0)
    head = jnp.where(row >= j, rolled[0:TAIL, :], pltpu.roll(prev_tail, j, 0))
    return jnp.concatenate([head, rolled[TAIL:, :]], axis=0)


def _in_kernel(x_ref, g_ref, w_ref, conv_ref, mu_ref, cos_ref, sin_ref,
               pm_ref, pr_ref, pw_ref, gt_ref, qk_tail, pw_tail, *, steps_per_seq):
    @pl.when(lax.rem(pl.program_id(0), steps_per_seq) == 0)
    def _():
        qk_tail[...] = jnp.zeros(qk_tail.shape, F32)
        pw_tail[...] = jnp.zeros(pw_tail.shape, F32)

    tm = x_ref.shape[0]
    hb = _bf(_rms(x_ref[...], g_ref[...]))
    c1, c2 = PROJ_SPLIT
    p_a = jnp.dot(hb, w_ref[:, 0:c1], preferred_element_type=F32)
    p_b = jnp.dot(hb, w_ref[:, c1:c2], preferred_element_type=F32)

    conv_w = conv_ref[...]
    qk_raw = p_a[:, 0:2 * M_WIDTH]
    acc = qk_raw * conv_w[3:4, :]
    for j in (1, 2, 3):
        acc = acc + _shift_rows(qk_raw, qk_tail[...], j) * conv_w[3 - j:4 - j, :]
    qk = _silu(acc)
    pm_ref[:, 0:M_WIDTH] = qk[:, 0:M_WIDTH]
    pm_ref[:, M_WIDTH:2 * M_WIDTH] = qk[:, M_WIDTH:2 * M_WIDTH] * (HEAD_DIM ** -0.5)
    pm_ref[:, 2 * M_WIDTH:PM_W] = p_a[:, 2 * M_WIDTH:PM_W]
    qk_tail[...] = qk_raw[tm - TAIL:tm, :]
    gt_ref[...] = jnp.transpose(p_a[:, 4 * M_WIDTH:PM_W])[0:GT_ROWS, :]

    p_c = jnp.dot(hb, w_ref[:, c2:P_W], preferred_element_type=F32)

    pr = jnp.concatenate([p_a[:, PM_W:c1], p_b[:, 0:PM_W + R_COLS - c1]], axis=1)
    cos, sin = cos_ref[...], sin_ref[...]
    q = pr[:, 0:R_WIDTH]
    k = pr[:, R_WIDTH:2 * R_WIDTH]
    pr_ref[:, 0:R_WIDTH] = q * cos + _rot_half(q) * sin
    pr_ref[:, R_WIDTH:2 * R_WIDTH] = (k * cos + _rot_half(k) * sin) * (HEAD_DIM ** -0.5)
    pr_ref[:, 2 * R_WIDTH:R_COLS] = pr[:, 2 * R_WIDTH:R_COLS]

    pw = jnp.concatenate([p_b[:, PM_W + R_COLS - c1:c2 - c1], p_c], axis=1)
    pw_ref[...] = pw + mu_ref[...] * (_shift_rows(pw, pw_tail[...], 1) - pw)
    pw_tail[...] = pw[tm - TAIL:tm, :]


def _layer_spec(arr, l, grid_rank):
    zeros = (0,) * (arr.ndim - 1)
    if grid_rank == 1:
        return pl.BlockSpec((None,) + arr.shape[1:], lambda i: (l,) + zeros)
    return pl.BlockSpec((None,) + arr.shape[1:], lambda b, c: (l,) + zeros)


def _in_proj(x2, g_all, w_pad_all, conv_all, mu_all, cos, sin, l, t, tm):
    n, d = x2.shape
    steps_per_seq = t // tm
    tab = pl.BlockSpec((tm, R_WIDTH), lambda i: (lax.rem(i, steps_per_seq), 0))
    return pl.pallas_call(
        functools.partial(_in_kernel, steps_per_seq=steps_per_seq),
        grid=(n // tm,),
        in_specs=[pl.BlockSpec((tm, d), lambda i: (i, 0)),
                  _layer_spec(g_all, l, 1),
                  _layer_spec(w_pad_all, l, 1),
                  _layer_spec(conv_all, l, 1),
                  _layer_spec(mu_all, l, 1),
                  tab, tab],
        out_specs=[pl.BlockSpec((tm, PM_W), lambda i: (i, 0)),
                   pl.BlockSpec((tm, R_COLS), lambda i: (i, 0)),
                   pl.BlockSpec((tm, W_COLS), lambda i: (i, 0)),
                   pl.BlockSpec((GT_ROWS, tm), lambda i: (0, i))],
        out_shape=[jax.ShapeDtypeStruct((n, PM_W), F32),
                   jax.ShapeDtypeStruct((n, R_COLS), F32),
                   jax.ShapeDtypeStruct((n, W_COLS), F32),
                   jax.ShapeDtypeStruct((GT_ROWS, n), F32)],
        scratch_shapes=[pltpu.VMEM((TAIL, 2 * M_WIDTH), F32),
                        pltpu.VMEM((TAIL, W_COLS), F32)],
        compiler_params=pltpu.CompilerParams(dimension_semantics=("arbitrary",),
                                             vmem_limit_bytes=VMEM_LIMIT),
        name="in_proj",
    )(x2, g_all, w_pad_all, conv_all, mu_all, cos, sin)


def _rows(c):
    return slice(c * CHUNK, (c + 1) * CHUNK)


def _grows(g):
    return slice(g * GROUP * CHUNK, (g + 1) * GROUP * CHUNK)


def _lanes(h):
    return slice(h * HEAD_DIM, (h + 1) * HEAD_DIM)


def _cummax_rows(x):
    row = lax.broadcasted_iota(jnp.int32, x.shape, 0)
    step = 1
    while step < x.shape[0]:
        x = jnp.maximum(x, jnp.where(row >= step, pltpu.roll(x, step, 0), -jnp.inf))
        step *= 2
    return x


def _head_sums(x, bd_bf, signed):
    outs = []
    off = 0
    while off < x.shape[1]:
        n = min(BD_W, x.shape[1] - off)
        b = bd_bf[0:n, 0:n]
        if signed:
            hi, lo = _split2(x[:, off:off + n])
            outs.append(jnp.dot(hi, b, preferred_element_type=F32) + jnp.dot(lo, b, preferred_element_type=F32))
        else:
            outs.append(jnp.dot(_bf(x[:, off:off + n]), b, preferred_element_type=F32))
        off += n
    return jnp.concatenate(outs, axis=1)


def _heads_ln(x, eps, bd_bf):
    mu = _head_sums(x, bd_bf, True) * (1.0 / HEAD_DIM)
    xc = x - mu
    var = _head_sums(xc * xc, bd_bf, False) * (1.0 / HEAD_DIM)
    return xc * lax.rsqrt(var + eps)


def _mlstm_heads(pm, gt, gate_bias, b_i_col, b_f_col, cn_scr, m_scr, pre_ref,
                 tril_bf, triu_g_bf, causal_g, nch):
    chunks = range(nch)
    groups = range(nch // GROUP)
    q = pm[:, 0:M_WIDTH]
    k = pm[:, M_WIDTH:2 * M_WIDTH]
    v = pm[:, 2 * M_WIDTH:3 * M_WIDTH]
    g128 = pm[:, 4 * M_WIDTH:4 * M_WIDTH + 128] + gate_bias
    lf128 = _log_sigmoid(g128)
    m = m_scr[...]
    b_al, cm, m_old, m_new, b_last = [], [], [], [], []
    for c in chunks:
        b_c = pltpu.roll(_dot_exact_rhs(tril_bf, lf128[_rows(c), :]), 128 - M_HEADS, 1)
        cm_c = _cummax_rows(g128[_rows(c), :] - b_c)
        bl = b_c[CHUNK - 1:CHUNK, :]
        b_al.append(b_c)
        cm.append(cm_c)
        b_last.append(bl)
        m_old.append(m)
        m = bl + jnp.maximum(m, cm_c[CHUNK - 1:CHUNK, :])
        m_new.append(m)
    m_scr[...] = m
    cat = lambda xs: jnp.concatenate(xs, axis=0)
    top = [jnp.maximum(m_old[c], cm[c]) for c in chunks]
    neg_top = cat([-t for t in top])
    inter_w = cat([jnp.exp(m_old[c] - top[c]) for c in chunks])
    e_mrow = cat([jnp.exp(-(b_al[c] + top[c])) for c in chunks])
    sw = cat([jnp.exp(b_last[c] - b_al[c] + g128[_rows(c), :] - m_new[c]) for c in chunks])
    dec = [jnp.exp(b_last[c] + m_old[c] - m_new[c]) for c in chunks]
    d_row = []
    for g in groups:
        i_r = gt[0:M_HEADS, _grows(g)] + b_i_col
        lf_r = _log_sigmoid(gt[M_HEADS:2 * M_HEADS, _grows(g)] + b_f_col)
        d_row.append(i_r - _dot_exact_lhs(lf_r, triu_g_bf))
    ones = jnp.ones((GROUP * CHUNK, HEAD_DIM), F32)

    def head(h):
        sl = _lanes(h)
        hc = slice(h, h + 1)
        qk_t = [_dot_nt(q[_grows(g), sl], k[_grows(g), sl]) for g in groups]
        vaug = [jnp.concatenate([v[_grows(g), sl], ones], axis=1) for g in groups]
        cn_in = [_dot_tn(k[_rows(c), sl] * sw[_rows(c), hc],
                         vaug[c // GROUP][_rows(c % GROUP), :]) for c in chunks]
        yield
        e = [jnp.exp(jnp.where(causal_g, neg_top[_grows(g), hc] + d_row[g][hc, :], -jnp.inf))
             for g in groups]
        sva = [_dot(qk_t[g] * e[g], vaug[g]) for g in groups]
        cn = cn_scr[h]
        qcn = []
        for c in chunks:
            qcn.append(_dot(q[_rows(c), sl], cn))
            cn = dec[c][:, hc] * cn + cn_in[c]
        cn_scr[h] = cn
        yield
        for g in groups:
            nd = sva[g] + inter_w[_grows(g), hc] * jnp.concatenate(qcn[g * GROUP:(g + 1) * GROUP], axis=0)
            pre_ref[_grows(g), sl] = nd[:, 0:HEAD_DIM] / jnp.maximum(
                jnp.abs(nd[:, HEAD_DIM:2 * HEAD_DIM]), e_mrow[_grows(g), hc])

    return [head(h) for h in range(M_HEADS)]


def _mlstm_finish(pm, pre, ln_g, bd_bf):
    o = pm[:, 3 * M_WIDTH:4 * M_WIDTH]
    return _sigmoid(o) * (_heads_ln(pre, HEAD_LN_EPS, bd_bf) * ln_g)


def _retention_heads(pr, dmat_ref, qdec, kdec, cdec, r_scr, pre_ref, nch):
    chunks = range(nch)
    groups = range(nch // GROUP)
    qr = pr[:, 0:R_WIDTH]
    kr = pr[:, R_WIDTH:2 * R_WIDTH]
    v = pr[:, 2 * R_WIDTH:3 * R_WIDTH]
    qd = qr * qdec
    kd = kr * kdec

    def head(h):
        sl = _lanes(h)
        qk_t = [_dot_nt(qr[_grows(g), sl], kr[_grows(g), sl]) for g in groups]
        r_in = [_dot_tn(kd[_rows(c), sl], v[_rows(c), sl]) for c in chunks]
        yield
        intra = [_dot(qk_t[g] * dmat_ref[h], v[_grows(g), sl]) for g in groups]
        r_st = r_scr[h]
        cross = []
        for c in chunks:
            cross.append(_dot(qd[_rows(c), sl], r_st))
            r_st = cdec[:, sl] * r_st + r_in[c]
        r_scr[h] = r_st
        yield
        for g in groups:
            pre_ref[_grows(g), sl] = intra[g] + jnp.concatenate(cross[g * GROUP:(g + 1) * GROUP], axis=0)

    return [head(h) for h in range(R_HEADS)]


def _retention_finish(pr, pre, bd_bf):
    return _silu(pr[:, 3 * R_WIDTH:4 * R_WIDTH]) * _heads_ln(pre, HEAD_LN_EPS, bd_bf)


def _rwkv_heads(ps, w0, w_up, a0, a_up, g_up, k_k, k_a, r_k,
                s_scr, pre_ref, tril_bf, bd_bf, lower_incl2, lower_strict2, eye_g, sub_masks, nch):
    chunks = range(nch)
    groups = range(nch // GROUP)
    rg_n = GROUP * CHUNK
    r = ps[:, 0:W_WIDTH]
    k = ps[:, W_WIDTH:2 * W_WIDTH]
    v = ps[:, 2 * W_WIDTH:3 * W_WIDTH]
    o1 = 3 * W_WIDTH
    wl = ps[:, o1:o1 + W_LORA]
    al = ps[:, o1 + W_LORA:o1 + W_LORA + A_LORA]
    gl = ps[:, o1 + W_LORA + A_LORA:o1 + W_LORA + A_LORA + G_LORA]
    lw = -math.exp(-0.5) * _sigmoid(w0 + _dot(jnp.tanh(wl), w_up))
    a = _sigmoid(a0 + _dot(al, a_up))
    g = _dot(_sigmoid(gl), g_up)
    kk0 = k * k_k
    kk = kk0 / jnp.maximum(jnp.sqrt(_head_sums(kk0 * kk0, bd_bf, False)), 1e-12)
    k2 = k * (1.0 + (a - 1.0) * k_a)
    cum_c = [_dot_exact_rhs(tril_bf, lw[_rows(c), :]) for c in chunks]
    cum = jnp.concatenate(cum_c, axis=0)
    to_end = jnp.concatenate([cum_c[c][CHUNK - 1:CHUNK, :] - cum_c[c] for c in chunks], axis=0)
    e_pos = jnp.exp(cum)
    e_neg = jnp.exp(-cum)
    e_last = jnp.exp(to_end)
    b_t = -(kk * a)
    a_prev = kk * jnp.exp(cum - lw)
    b_neg = b_t * e_neg
    b_end = b_t * e_last
    r_pos = r * e_pos
    k_neg = k2 * e_neg
    k_end = k2 * e_last
    bonus_v = _head_sums(r * k2 * r_k, bd_bf, True) * v

    def head(h):
        sl = _lanes(h)
        ar = [jnp.concatenate([a_prev[_grows(q_), sl], r_pos[_grows(q_), sl]], axis=0) for q_ in groups]
        bk = [jnp.concatenate([b_neg[_grows(q_), sl], k_neg[_grows(q_), sl]], axis=0) for q_ in groups]
        vg = [v[_grows(q_), sl] for q_ in groups]
        m = [_dot_nt(ar[q_], bk[q_]) for q_ in groups]
        yield
        top = [jnp.where(lower_strict2, m[q_][0:rg_n, :], 0.0) for q_ in groups]
        bot = [jnp.where(lower_incl2, m[q_][rg_n:2 * rg_n, :], 0.0) for q_ in groups]
        a_ab = [top[q_][:, 0:rg_n] for q_ in groups]
        akv = [_dot(top[q_][:, rg_n:2 * rg_n], vg[q_]) for q_ in groups]
        yield
        t_inv = [eye_g + jnp.where(sub_masks[0], a_ab[q_], 0.0) for q_ in groups]
        for lvl in range(1, len(sub_masks)):
            at = [_dot(jnp.where(sub_masks[lvl], a_ab[q_], 0.0), t_inv[q_]) for q_ in groups]
            yield
            t_inv = [t_inv[q_] + _dot(t_inv[q_], at[q_]) for q_ in groups]
            yield
        x = [_dot(t_inv[q_], jnp.concatenate([ar[q_][0:rg_n, :], akv[q_]], axis=1)) for q_ in groups]
        yield
        zeros = jnp.zeros((CHUNK, HEAD_DIM), F32)
        eg = []
        for c in chunks:
            rs = _rows(c)
            lhs = jnp.concatenate([x[c // GROUP][_rows(c % GROUP), :],
                                   jnp.concatenate([zeros, v[rs, sl]], axis=1)], axis=0)
            bk_end = jnp.concatenate([b_end[rs, sl], k_end[rs, sl]], axis=0)
            eg.append(_dot_tn(lhs, bk_end))
        yield
        s_st = s_scr[h]
        for q_ in groups:
            us, rss = [], []
            for j in range(GROUP):
                c = q_ * GROUP + j
                rs = _rows(c)
                xj = x[q_][_rows(j), :]
                wr = jnp.concatenate([xj[:, 0:HEAD_DIM], r_pos[rs, sl]], axis=0)
                gs = _dot_nt(wr, s_st)
                se = _dot(s_st, eg[c][0:HEAD_DIM, :])
                yield
                s_st = s_st * e_pos[(c + 1) * CHUNK - 1:(c + 1) * CHUNK, sl] + se + eg[c][HEAD_DIM:2 * HEAD_DIM, :]
                us.append(gs[0:CHUNK, :] + xj[:, HEAD_DIM:2 * HEAD_DIM])
                rss.append(gs[CHUNK:2 * CHUNK, :])
            uvg = jnp.concatenate(us + [vg[q_]], axis=0)
            pre_ref[_grows(q_), sl] = jnp.concatenate(rss, axis=0) + _dot(bot[q_], uvg)
        s_scr[h] = s_st

    return [head(h) for h in range(W_HEADS)], (g, bonus_v)


def _rwkv_finish(pre, g, bonus_v, ln_g, ln_b, bd_bf):
    return (_heads_ln(pre, RWKV_LN_EPS, bd_bf) * ln_g + ln_b + bonus_v) * g


def _run_interleaved(gens):
    pending = list(gens)
    while pending:
        for g in list(pending):
            try:
                next(g)
            except StopIteration:
                pending.remove(g)


def _mix_kernel(pm_ref, pr_ref, pw_ref, gt_ref, qdec_ref, kdec_ref, dmat_ref,
                cdec_ref, gbias_ref, bi_col_ref, bf_col_ref, mln_ref,
                w0_ref, wup_ref, a0_ref, aup_ref, gup_ref, kk_ref, ka_ref, rk_ref,
                wlng_ref, wlnb_ref,
                y_ref,
                cn_scr, m_scr, r_scr, s_scr, prem_scr, prer_scr, prew_scr,
                *, nch):
    @pl.when(pl.program_id(1) == 0)
    def _():
        cn_scr[...] = jnp.zeros(cn_scr.shape, F32)
        m_scr[...] = jnp.full(m_scr.shape, M_INIT, F32)
        r_scr[...] = jnp.zeros(r_scr.shape, F32)
        s_scr[...] = jnp.zeros(s_scr.shape, F32)

    rg_n = GROUP * CHUNK
    row = lax.broadcasted_iota(jnp.int32, (CHUNK, CHUNK), 0)
    col = lax.broadcasted_iota(jnp.int32, (CHUNK, CHUNK), 1)
    tril_bf = jnp.where(col <= row, 1.0, 0.0).astype(BF16)
    rowb = lax.broadcasted_iota(jnp.int32, (BD_W, BD_W), 0)
    colb = lax.broadcasted_iota(jnp.int32, (BD_W, BD_W), 1)
    bd_bf = jnp.where((rowb >> 6) == (colb >> 6), 1.0, 0.0).astype(BF16)
    row2 = lax.broadcasted_iota(jnp.int32, (rg_n, 2 * rg_n), 0)
    col2 = lax.broadcasted_iota(jnp.int32, (rg_n, 2 * rg_n), 1) & (rg_n - 1)
    start2 = (row2 >> 6) << 6
    lower_incl2 = (col2 <= row2) & (col2 >= start2)
    lower_strict2 = (col2 < row2) & (col2 >= start2)
    causal_g = lower_incl2[:, 0:rg_n]
    rowg = lax.broadcasted_iota(jnp.int32, (rg_n, rg_n), 0)
    colg = lax.broadcasted_iota(jnp.int32, (rg_n, rg_n), 1)
    triu_g_bf = jnp.where((rowg <= colg) & (rowg >= ((colg >> 6) << 6)), 1.0, 0.0).astype(BF16)
    eye_g = jnp.where(rowg == colg, 1.0, 0.0).astype(F32)
    sub_masks = []
    for k in range(6):
        same_pair = ((rowg ^ colg) >> (k + 1)) == 0
        sub_masks.append(same_pair & (((rowg >> k) & 1) == 1) & (((colg >> k) & 1) == 0))

    w_gens, (w_gate, bonus_v) = _rwkv_heads(
        pw_ref[0], w0_ref[...], wup_ref[...], a0_ref[...], aup_ref[...],
        gup_ref[...], kk_ref[...], ka_ref[...], rk_ref[...], s_scr, prew_scr, tril_bf, bd_bf,
        lower_incl2, lower_strict2, eye_g, sub_masks, nch)
    pm = pm_ref[0]
    m_gens = _mlstm_heads(pm, gt_ref[...], gbias_ref[...], bi_col_ref[...], bf_col_ref[...],
                          cn_scr, m_scr, prem_scr, tril_bf, triu_g_bf, causal_g, nch)
    pr = pr_ref[0]
    r_gens = _retention_heads(pr, dmat_ref, qdec_ref[...], kdec_ref[...], cdec_ref[...],
                              r_scr, prer_scr, nch)
    _run_interleaved(w_gens + m_gens + r_gens)

    y_ref[0, :, 0:M_WIDTH] = _mlstm_finish(pm, prem_scr[...], mln_ref[...], bd_bf).astype(y_ref.dtype)
    y_ref[0, :, M_WIDTH:M_WIDTH + R_WIDTH] = _retention_finish(pr, prer_scr[...], bd_bf).astype(y_ref.dtype)
    y_ref[0, :, M_WIDTH + R_WIDTH:M_WIDTH + R_WIDTH + W_WIDTH] = _rwkv_finish(
        prew_scr[...], w_gate, bonus_v, wlng_ref[...], wlnb_ref[...], bd_bf).astype(y_ref.dtype)


def _full(shape):
    nd = len(shape)
    return pl.BlockSpec(shape, lambda b, c: (0,) * nd)


def _mixers(pm, pr, pw, gt, tabs, layer_params, l, nch):
    bsz, t, _ = pm.shape
    tt = nch * CHUNK
    nblk = t // tt
    tab_args = list(tabs)
    args = [pm, pr, pw, gt] + tab_args + list(layer_params)
    in_specs = [
        pl.BlockSpec((1, tt, PM_W), lambda b, c: (b, c, 0)),
        pl.BlockSpec((1, tt, R_COLS), lambda b, c: (b, c, 0)),
        pl.BlockSpec((1, tt, W_COLS), lambda b, c: (b, c, 0)),
        pl.BlockSpec((GT_ROWS, tt), lambda b, c: (0, b * nblk + c)),
    ] + [pl.BlockSpec((tt, R_WIDTH), lambda b, c: (c, 0)) for _ in range(2)] + [
        _full(a.shape) for a in tab_args[2:]] + [_layer_spec(a, l, 2) for a in layer_params]
    d_out = M_WIDTH + R_WIDTH + W_WIDTH
    return pl.pallas_call(
        functools.partial(_mix_kernel, nch=nch),
        grid=(bsz, t // tt),
        in_specs=in_specs,
        out_specs=pl.BlockSpec((1, tt, d_out), lambda b, c: (b, c, 0)),
        out_shape=jax.ShapeDtypeStruct((bsz, t, d_out), BF16),
        scratch_shapes=[
            pltpu.VMEM((M_HEADS, HEAD_DIM, 2 * HEAD_DIM), F32),
            pltpu.VMEM((1, 128), F32),
            pltpu.VMEM((R_HEADS, HEAD_DIM, HEAD_DIM), F32),
            pltpu.VMEM((W_HEADS, HEAD_DIM, HEAD_DIM), F32),
            pltpu.VMEM((tt, M_WIDTH), F32),
            pltpu.VMEM((tt, R_WIDTH), F32),
            pltpu.VMEM((tt, W_WIDTH), F32),
        ],
        compiler_params=pltpu.CompilerParams(dimension_semantics=("arbitrary", "arbitrary"),
                                             vmem_limit_bytes=VMEM_LIMIT),
        name="mixers",
    )(*args)


def _ffn_kernel(y_ref, wo_ref, x_ref, g_ref, wg_ref, wu_ref, wd_ref, gf_ref, o_ref, *, ff_chunk, final_norm):
    x = x_ref[...] + jnp.dot(y_ref[...], wo_ref[...], preferred_element_type=F32)
    hb = _bf(_rms(x, g_ref[...]))
    acc = x
    d_ff = wg_ref.shape[1]
    for c in range(d_ff // ff_chunk):
        cs = slice(c * ff_chunk, (c + 1) * ff_chunk)
        gate = jnp.dot(hb, wg_ref[:, cs], preferred_element_type=F32)
        up = jnp.dot(hb, wu_ref[:, cs], preferred_element_type=F32)
        acc = acc + jnp.dot(_bf(_silu(gate) * up), wd_ref[cs, :], preferred_element_type=F32)
    if final_norm:
        acc = _rms(acc, gf_ref[...])
    o_ref[...] = acc


def _out_ffn(y2, x2, wo_all, g_all, wg_all, wu_all, wd_all, gfin, l, tm, final_norm):
    n, d = x2.shape
    kern = functools.partial(_ffn_kernel, ff_chunk=FF_CHUNK, final_norm=final_norm)
    return pl.pallas_call(
        kern,
        grid=(n // tm,),
        in_specs=[pl.BlockSpec((tm, y2.shape[1]), lambda i: (i, 0)),
                  _layer_spec(wo_all, l, 1),
                  pl.BlockSpec((tm, d), lambda i: (i, 0)),
                  _layer_spec(g_all, l, 1),
                  _layer_spec(wg_all, l, 1),
                  _layer_spec(wu_all, l, 1),
                  _layer_spec(wd_all, l, 1),
                  pl.BlockSpec((1, d), lambda i: (0, 0))],
        out_specs=pl.BlockSpec((tm, d), lambda i: (i, 0)),
        out_shape=jax.ShapeDtypeStruct((n, d), F32),
        compiler_params=pltpu.CompilerParams(dimension_semantics=("arbitrary",),
                                             vmem_limit_bytes=VMEM_LIMIT),
        name="out_ffn",
    )(y2, wo_all, x2, g_all, wg_all, wu_all, wd_all, gfin)


def _retention_tables(t):
    h, d = R_HEADS, HEAD_DIM
    pos = jnp.arange(t, dtype=F32)
    theta = 1.0 / (ROPE_BASE ** jnp.linspace(0.0, 1.0, d // 2, dtype=F32))
    ang = pos[:, None] * theta[None, :]
    cos, sin = jnp.cos(ang), jnp.sin(ang)
    cos_t = jnp.tile(jnp.concatenate([cos, cos], axis=1), (1, h))
    sin_t = jnp.tile(jnp.concatenate([-sin, sin], axis=1), (1, h))
    log_g = jnp.log(1.0 - 2.0 ** (-5.0 - jnp.arange(h, dtype=F32)))
    idx = jnp.arange(CHUNK, dtype=F32)
    diff = idx[:, None] - idx[None, :]
    dmat = jnp.where(diff >= 0, jnp.exp(jnp.maximum(diff, 0.0)[None] * log_g[:, None, None]), 0.0)
    q_dec = jnp.exp((idx[None, :] + 1.0) * log_g[:, None])
    k_dec = jnp.exp((CHUNK - 1.0 - idx[None, :]) * log_g[:, None])
    c_dec = jnp.exp(CHUNK * log_g)
    expand = lambda z: jnp.repeat(z.T, d, axis=1)
    eye_g = jnp.eye(GROUP, dtype=F32)
    dmat_g = jnp.einsum('ab,hij->haibj', eye_g, dmat).reshape(h, GROUP * CHUNK, GROUP * CHUNK)
    reps = t // CHUNK
    return (cos_t, sin_t, jnp.tile(expand(q_dec), (reps, 1)), jnp.tile(expand(k_dec), (reps, 1)),
            dmat_g, jnp.repeat(c_dec, d)[None, :])


def kernel(x, w_in, ln1_g, ln2_g, lnf_g, m_conv, m_b_i, m_b_f, m_ln_g, rw_mu, rw_w0, rw_w_up,
           rw_a0, rw_a_up, rw_g_up, rw_k_k, rw_k_a, rw_r_k, rw_ln_g, rw_ln_b, w_out, w_gate,
           w_up, w_down):
    bsz, t, d = x.shape
    depth = w_in.shape[0]
    n = bsz * t
    nc = t // CHUNK
    tm = min(512, n)
    nch = math.gcd(nc, NCH)
    tabs = _retention_tables(t)
    x2 = x.reshape(n, d)
    rows3 = lambda z: z.reshape(depth, 1, -1)
    cols3 = lambda z: z.reshape(depth, -1, 1)
    w_pad = jnp.concatenate(
        [w_in[:, :, :M_COLS], jnp.zeros((depth, d, PM_W - M_COLS), w_in.dtype), w_in[:, :, M_COLS:]],
        axis=2).astype(BF16)
    gate_bias = jnp.concatenate(
        [m_b_i, m_b_f, jnp.zeros((depth, 128 - 2 * M_HEADS), F32)], axis=1)[:, None, :]
    layer_params = (gate_bias, cols3(m_b_i), cols3(m_b_f), rows3(m_ln_g),
                    rows3(rw_w0), rw_w_up, rows3(rw_a0), rw_a_up, rw_g_up, rows3(rw_k_k),
                    rows3(rw_k_a), rows3(rw_r_k), rows3(rw_ln_g), rows3(rw_ln_b))
    wo_bf, wg_bf, wu_bf, wd_bf = (w.astype(BF16) for w in (w_out, w_gate, w_up, w_down))
    ln1, ln2 = rows3(ln1_g), rows3(ln2_g)
    for l in range(depth):
        pm, pr, pw, gt = _in_proj(x2, ln1, w_pad, m_conv, rows3(rw_mu), tabs[0], tabs[1], l, t,
                                  math.gcd(t, tm))
        y = _mixers(pm.reshape(bsz, t, PM_W), pr.reshape(bsz, t, R_COLS), pw.reshape(bsz, t, W_COLS),
                    gt, tabs[2:], layer_params, l, nch)
        x2 = _out_ffn(y.reshape(n, d), x2, wo_bf, ln2, wg_bf, wu_bf, wd_bf, lnf_g.reshape(1, -1), l, tm,
                      final_norm=(l == depth - 1))
    return x2.reshape(bsz, t, d)
```

```python
import functools
import math

import jax
import jax.numpy as jnp
from jax import lax
from jax.experimental import pallas as pl
from jax.experimental.pallas import tpu as pltpu

F32 = jnp.float32
BF16 = jnp.bfloat16

CHUNK = 64
HEAD_DIM = 64
M_HEADS, R_HEADS, W_HEADS = 4, 6, 6
M_WIDTH, R_WIDTH, W_WIDTH = 256, 384, 384
W_LORA, A_LORA, G_LORA = 64, 64, 128
M_COLS = 4 * M_WIDTH + 2 * M_HEADS
R_COLS = 4 * R_WIDTH
W_COLS = 3 * W_WIDTH + W_LORA + A_LORA + G_LORA
PM_W = 4 * M_WIDTH + 128
P_W = PM_W + R_COLS + W_COLS
RMS_EPS = 1e-6
HEAD_LN_EPS = 1e-5
RWKV_LN_EPS = 64e-5
ROPE_BASE = 10000.0
M_INIT = -1e30
VMEM_LIMIT = 56 * 1024 * 1024
TAIL = 8
NCH = 8
GROUP = 2
BD_W = 256
FF_CHUNK = 256
PROJ_SPLIT = (1280, 2816)
GT_ROWS = 8


def _bf(x):
    return x.astype(BF16)


def _dot(a, b):
    return jnp.dot(_bf(a), _bf(b), preferred_element_type=F32)


def _dot_nt(a, b):
    return lax.dot_general(_bf(a), _bf(b), (((1,), (1,)), ((), ())), preferred_element_type=F32)


def _dot_tn(a, b):
    return lax.dot_general(_bf(a), _bf(b), (((0,), (0,)), ((), ())), preferred_element_type=F32)


def _split2(x):
    hi = _bf(x)
    lo = _bf(x - hi.astype(F32))
    return hi, lo


def _split3(x):
    x1 = _bf(x)
    r = x - x1.astype(F32)
    x2 = _bf(r)
    x3 = _bf(r - x2.astype(F32))
    return x1, x2, x3


def _dot_exact_rhs(m_bf, x):
    x1, x2, x3 = _split3(x)
    return (jnp.dot(m_bf, x1, preferred_element_type=F32)
            + jnp.dot(m_bf, x2, preferred_element_type=F32)
            + jnp.dot(m_bf, x3, preferred_element_type=F32))


def _dot_exact_lhs(x, m_bf):
    x1, x2, x3 = _split3(x)
    return (jnp.dot(x1, m_bf, preferred_element_type=F32)
            + jnp.dot(x2, m_bf, preferred_element_type=F32)
            + jnp.dot(x3, m_bf, preferred_element_type=F32))


def _sigmoid(x):
    return 1.0 / (1.0 + jnp.exp(-x))


def _silu(x):
    return x * _sigmoid(x)


def _softplus(x):
    return jnp.maximum(x, 0.0) + jnp.log(1.0 + jnp.exp(-jnp.abs(x)))


def _log_sigmoid(x):
    return -_softplus(-x)


def _rms(x, g):
    return x * lax.rsqrt(jnp.mean(x * x, axis=-1, keepdims=True) + RMS_EPS) * g


def _rot_half(x):
    pieces = []
    for i in range(x.shape[1] // 128):
        p = x[:, 128 * i:128 * (i + 1)]
        lane = lax.broadcasted_iota(jnp.int32, p.shape, 1)
        pieces.append(jnp.where((lane & 63) < 32, pltpu.roll(p, 96, 1), pltpu.roll(p, 32, 1)))
    return jnp.concatenate(pieces, axis=1)


def _shift_rows(cur, prev_tail, j):
    rolled = pltpu.roll(cur, j, 0)
    row = lax.broadcasted_iota(jnp.int32, prev_tail.shape, 0)
    head = jnp.where(row >= j, rolled[0:TAIL, :], pltpu.roll(prev_tail, j, 0))
    return jnp.concatenate([head, rolled[TAIL:, :]], axis=0)


def _in_kernel(x_ref, g_ref, w_ref, conv_ref, mu_ref, cos_ref, sin_ref,
               pm_ref, pr_ref, pw_ref, gt_ref, qk_tail, pw_tail, *, steps_per_seq):
    @pl.when(lax.rem(pl.program_id(0), steps_per_seq) == 0)
    def _():
        qk_tail[...] = jnp.zeros(qk_tail.shape, F32)
        pw_tail[...] = jnp.zeros(pw_tail.shape, F32)

    tm = x_ref.shape[0]
    hb = _bf(_rms(x_ref[...], g_ref[...]))
    c1, c2 = PROJ_SPLIT
    p_a = jnp.dot(hb, w_ref[:, 0:c1], preferred_element_type=F32)
    p_b = jnp.dot(hb, w_ref[:, c1:c2], preferred_element_type=F32)

    conv_w = conv_ref[...]
    qk_raw = p_a[:, 0:2 * M_WIDTH]
    acc = qk_raw * conv_w[3:4, :]
    for j in (1, 2, 3):
        acc = acc + _shift_rows(qk_raw, qk_tail[...], j) * conv_w[3 - j:4 - j, :]
    qk = _silu(acc)
    pm_ref[:, 0:M_WIDTH] = qk[:, 0:M_WIDTH]
    pm_ref[:, M_WIDTH:2 * M_WIDTH] = qk[:, M_WIDTH:2 * M_WIDTH] * (HEAD_DIM ** -0.5)
    pm_ref[:, 2 * M_WIDTH:PM_W] = p_a[:, 2 * M_WIDTH:PM_W]
    qk_tail[...] = qk_raw[tm - TAIL:tm, :]
    gt_ref[...] = jnp.transpose(p_a[:, 4 * M_WIDTH:PM_W])[0:GT_ROWS, :]

    p_c = jnp.dot(hb, w_ref[:, c2:P_W], preferred_element_type=F32)

    pr = jnp.concatenate([p_a[:, PM_W:c1], p_b[:, 0:PM_W + R_COLS - c1]], axis=1)
    cos, sin = cos_ref[...], sin_ref[...]
    q = pr[:, 0:R_WIDTH]
    k = pr[:, R_WIDTH:2 * R_WIDTH]
    pr_ref[:, 0:R_WIDTH] = q * cos + _rot_half(q) * sin
    pr_ref[:, R_WIDTH:2 * R_WIDTH] = (k * cos + _rot_half(k) * sin) * (HEAD_DIM ** -0.5)
    pr_ref[:, 2 * R_WIDTH:R_COLS] = pr[:, 2 * R_WIDTH:R_COLS]

    pw = jnp.concatenate([p_b[:, PM_W + R_COLS - c1:c2 - c1], p_c], axis=1)
    pw_ref[...] = pw + mu_ref[...] * (_shift_rows(pw, pw_tail[...], 1) - pw)
    pw_tail[...] = pw[tm - TAIL:tm, :]


def _layer_spec(arr, l, grid_rank):
    zeros = (0,) * (arr.ndim - 1)
    if grid_rank == 1:
        return pl.BlockSpec((None,) + arr.shape[1:], lambda i: (l,) + zeros)
    return pl.BlockSpec((None,) + arr.shape[1:], lambda b, c: (l,) + zeros)


def _in_proj(x2, g_all, w_pad_all, conv_all, mu_all, cos, sin, l, t, tm):
    n, d = x2.shape
    steps_per_seq = t // tm
    tab = pl.BlockSpec((tm, R_WIDTH), lambda i: (lax.rem(i, steps_per_seq), 0))
    return pl.pallas_call(
        functools.partial(_in_kernel, steps_per_seq=steps_per_seq),
        grid=(n // tm,),
        in_specs=[pl.BlockSpec((tm, d), lambda i: (i, 0)),
                  _layer_spec(g_all, l, 1),
                  _layer_spec(w_pad_all, l, 1),
                  _layer_spec(conv_all, l, 1),
                  _layer_spec(mu_all, l, 1),
                  tab, tab],
        out_specs=[pl.BlockSpec((tm, PM_W), lambda i: (i, 0)),
                   pl.BlockSpec((tm, R_COLS), lambda i: (i, 0)),
                   pl.BlockSpec((tm, W_COLS), lambda i: (i, 0)),
                   pl.BlockSpec((GT_ROWS, tm), lambda i: (0, i))],
        out_shape=[jax.ShapeDtypeStruct((n, PM_W), F32),
                   jax.ShapeDtypeStruct((n, R_COLS), F32),
                   jax.ShapeDtypeStruct((n, W_COLS), F32),
                   jax.ShapeDtypeStruct((GT_ROWS, n), F32)],
        scratch_shapes=[pltpu.VMEM((TAIL, 2 * M_WIDTH), F32),
                        pltpu.VMEM((TAIL, W_COLS), F32)],
        compiler_params=pltpu.CompilerParams(dimension_semantics=("arbitrary",),
                                             vmem_limit_bytes=VMEM_LIMIT),
        name="in_proj",
    )(x2, g_all, w_pad_all, conv_all, mu_all, cos, sin)


def _rows(c):
    return slice(c * CHUNK, (c + 1) * CHUNK)


def _grows(g):
    return slice(g * GROUP * CHUNK, (g + 1) * GROUP * CHUNK)


def _lanes(h):
    return slice(h * HEAD_DIM, (h + 1) * HEAD_DIM)


def _cummax_rows(x):
    row = lax.broadcasted_iota(jnp.int32, x.shape, 0)
    step = 1
    while step < x.shape[0]:
        x = jnp.maximum(x, jnp.where(row >= step, pltpu.roll(x, step, 0), -jnp.inf))
        step *= 2
    return x


def _head_sums(x, bd_bf, signed):
    outs = []
    off = 0
    while off < x.shape[1]:
        n = min(BD_W, x.shape[1] - off)
        b = bd_bf[0:n, 0:n]
        if signed:
            hi, lo = _split2(x[:, off:off + n])
            outs.append(jnp.dot(hi, b, preferred_element_type=F32) + jnp.dot(lo, b, preferred_element_type=F32))
        else:
            outs.append(jnp.dot(_bf(x[:, off:off + n]), b, preferred_element_type=F32))
        off += n
    return jnp.concatenate(outs, axis=1)


def _heads_ln(x, eps, bd_bf):
    mu = _head_sums(x, bd_bf, True) * (1.0 / HEAD_DIM)
    xc = x - mu
    var = _head_sums(xc * xc, bd_bf, False) * (1.0 / HEAD_DIM)
    return xc * lax.rsqrt(var + eps)


def _mlstm_heads(pm, gt, gate_bias, b_i_col, b_f_col, cn_scr, m_scr, pre_ref,
                 tril_bf, triu_g_bf, causal_g, nch):
    chunks = range(nch)
    groups = range(nch // GROUP)
    q = pm[:, 0:M_WIDTH]
    k = pm[:, M_WIDTH:2 * M_WIDTH]
    v = pm[:, 2 * M_WIDTH:3 * M_WIDTH]
    g128 = pm[:, 4 * M_WIDTH:4 * M_WIDTH + 128] + gate_bias
    lf128 = _log_sigmoid(g128)
    m = m_scr[...]
    b_al, cm, m_old, m_new, b_last = [], [], [], [], []
    for c in chunks:
        b_c = pltpu.roll(_dot_exact_rhs(tril_bf, lf128[_rows(c), :]), 128 - M_HEADS, 1)
        cm_c = _cummax_rows(g128[_rows(c), :] - b_c)
        bl = b_c[CHUNK - 1:CHUNK, :]
        b_al.append(b_c)
        cm.append(cm_c)
        b_last.append(bl)
        m_old.append(m)
        m = bl + jnp.maximum(m, cm_c[CHUNK - 1:CHUNK, :])
        m_new.append(m)
    m_scr[...] = m
    cat = lambda xs: jnp.concatenate(xs, axis=0)
    top = [jnp.maximum(m_old[c], cm[c]) for c in chunks]
    neg_top = cat([-t for t in top])
    inter_w = cat([jnp.exp(m_old[c] - top[c]) for c in chunks])
    e_mrow = cat([jnp.exp(-(b_al[c] + top[c])) for c in chunks])
    sw = cat([jnp.exp(b_last[c] - b_al[c] + g128[_rows(c), :] - m_new[c]) for c in chunks])
    dec = [jnp.exp(b_last[c] + m_old[c] - m_new[c]) for c in chunks]
    d_row = []
    for g in groups:
        i_r = gt[0:M_HEADS, _grows(g)] + b_i_col
        lf_r = _log_sigmoid(gt[M_HEADS:2 * M_HEADS, _grows(g)] + b_f_col)
        d_row.append(i_r - _dot_exact_lhs(lf_r, triu_g_bf))
    ones = jnp.ones((GROUP * CHUNK, HEAD_DIM), F32)

    def head(h):
        sl = _lanes(h)
        hc = slice(h, h + 1)
        qk_t = [_dot_nt(q[_grows(g), sl], k[_grows(g), sl]) for g in groups]
        vaug = [jnp.concatenate([v[_grows(g), sl], ones], axis=1) for g in groups]
        cn_in = [_dot_tn(k[_rows(c), sl] * sw[_rows(c), hc],
                         vaug[c // GROUP][_rows(c % GROUP), :]) for c in chunks]
        yield
        e = [jnp.exp(jnp.where(causal_g, neg_top[_grows(g), hc] + d_row[g][hc, :], -jnp.inf))
             for g in groups]
        sva = [_dot(qk_t[g] * e[g], vaug[g]) for g in groups]
        cn = cn_scr[h]
        qcn = []
        for c in chunks:
            qcn.append(_dot(q[_rows(c), sl], cn))
            cn = dec[c][:, hc] * cn + cn_in[c]
        cn_scr[h] = cn
        yield
        for g in groups:
            nd = sva[g] + inter_w[_grows(g), hc] * jnp.concatenate(qcn[g * GROUP:(g + 1) * GROUP], axis=0)
            pre_ref[_grows(g), sl] = nd[:, 0:HEAD_DIM] / jnp.maximum(
                jnp.abs(nd[:, HEAD_DIM:2 * HEAD_DIM]), e_mrow[_grows(g), hc])

    return [head(h) for h in range(M_HEADS)]


def _mlstm_finish(pm, pre, ln_g, bd_bf):
    o = pm[:, 3 * M_WIDTH:4 * M_WIDTH]
    return _sigmoid(o) * (_heads_ln(pre, HEAD_LN_EPS, bd_bf) * ln_g)


def _retention_heads(pr, dmat_ref, qdec, kdec, cdec, r_scr, pre_ref, nch):
    chunks = range(nch)
    groups = range(nch // GROUP)
    qr = pr[:, 0:R_WIDTH]
    kr = pr[:, R_WIDTH:2 * R_WIDTH]
    v = pr[:, 2 * R_WIDTH:3 * R_WIDTH]
    qd = qr * qdec
    kd = kr * kdec

    def head(h):
        sl = _lanes(h)
        qk_t = [_dot_nt(qr[_grows(g), sl], kr[_grows(g), sl]) for g in groups]
        r_in = [_dot_tn(kd[_rows(c), sl], v[_rows(c), sl]) for c in chunks]
        yield
        intra = [_dot(qk_t[g] * dmat_ref[h], v[_grows(g), sl]) for g in groups]
        r_st = r_scr[h]
        cross = []
        for c in chunks:
            cross.append(_dot(qd[_rows(c), sl], r_st))
            r_st = cdec[:, sl] * r_st + r_in[c]
        r_scr[h] = r_st
        yield
        for g in groups:
            pre_ref[_grows(g), sl] = intra[g] + jnp.concatenate(cross[g * GROUP:(g + 1) * GROUP], axis=0)

    return [head(h) for h in range(R_HEADS)]


def _retention_finish(pr, pre, bd_bf):
    return _silu(pr[:, 3 * R_WIDTH:4 * R_WIDTH]) * _heads_ln(pre, HEAD_LN_EPS, bd_bf)


def _rwkv_heads(ps, w0, w_up, a0, a_up, g_up, k_k, k_a, r_k,
                s_scr, pre_ref, tril_bf, bd_bf, lower_incl2, lower_strict2, eye_g, sub_masks, nch):
    chunks = range(nch)
    groups = range(nch // GROUP)
    rg_n = GROUP * CHUNK
    r = ps[:, 0:W_WIDTH]
    k = ps[:, W_WIDTH:2 * W_WIDTH]
    v = ps[:, 2 * W_WIDTH:3 * W_WIDTH]
    o1 = 3 * W_WIDTH
    wl = ps[:, o1:o1 + W_LORA]
    al = ps[:, o1 + W_LORA:o1 + W_LORA + A_LORA]
    gl = ps[:, o1 + W_LORA + A_LORA:o1 + W_LORA + A_LORA + G_LORA]
    lw = -math.exp(-0.5) * _sigmoid(w0 + _dot(jnp.tanh(wl), w_up))
    a = _sigmoid(a0 + _dot(al, a_up))
    g = _dot(_sigmoid(gl), g_up)
    kk0 = k * k_k
    kk = kk0 / jnp.maximum(jnp.sqrt(_head_sums(kk0 * kk0, bd_bf, False)), 1e-12)
    k2 = k * (1.0 + (a - 1.0) * k_a)
    cum_c = [_dot_exact_rhs(tril_bf, lw[_rows(c), :]) for c in chunks]
    cum = jnp.concatenate(cum_c, axis=0)
    to_end = jnp.concatenate([cum_c[c][CHUNK - 1:CHUNK, :] - cum_c[c] for c in chunks], axis=0)
    e_pos = jnp.exp(cum)
    e_neg = jnp.exp(-cum)
    e_last = jnp.exp(to_end)
    b_t = -(kk * a)
    a_prev = kk * jnp.exp(cum - lw)
    b_neg = b_t * e_neg
    b_end = b_t * e_last
    r_pos = r * e_pos
    k_neg = k2 * e_neg
    k_end = k2 * e_last
    bonus_v = _head_sums(r * k2 * r_k, bd_bf, True) * v

    def head(h):
        sl = _lanes(h)
        ar = [jnp.concatenate([a_prev[_grows(q_), sl], r_pos[_grows(q_), sl]], axis=0) for q_ in groups]
        bk = [jnp.concatenate([b_neg[_grows(q_), sl], k_neg[_grows(q_), sl]], axis=0) for q_ in groups]
        vg = [v[_grows(q_), sl] for q_ in groups]
        m = [_dot_nt(ar[q_], bk[q_]) for q_ in groups]
        yield
        top = [jnp.where(lower_strict2, m[q_][0:rg_n, :], 0.0) for q_ in groups]
        bot = [jnp.where(lower_incl2, m[q_][rg_n:2 * rg_n, :], 0.0) for q_ in groups]
        a_ab = [top[q_][:, 0:rg_n] for q_ in groups]
        akv = [_dot(top[q_][:, rg_n:2 * rg_n], vg[q_]) for q_ in groups]
        yield
        t_inv = [eye_g + jnp.where(sub_masks[0], a_ab[q_], 0.0) for q_ in groups]
        for lvl in range(1, len(sub_masks)):
            at = [_dot(jnp.where(sub_masks[lvl], a_ab[q_], 0.0), t_inv[q_]) for q_ in groups]
            yield
            t_inv = [t_inv[q_] + _dot(t_inv[q_], at[q_]) for q_ in groups]
            yield
        x = [_dot(t_inv[q_], jnp.concatenate([ar[q_][0:rg_n, :], akv[q_]], axis=1)) for q_ in groups]
        yield
        zeros = jnp.zeros((CHUNK, HEAD_DIM), F32)
        eg = []
        for c in chunks:
            rs = _rows(c)
            lhs = jnp.concatenate([x[c // GROUP][_rows(c % GROUP), :],
                                   jnp.concatenate([zeros, v[rs, sl]], axis=1)], axis=0)
            bk_end = jnp.concatenate([b_end[rs, sl], k_end[rs, sl]], axis=0)
            eg.append(_dot_tn(lhs, bk_end))
        yield
        s_st = s_scr[h]
        for q_ in groups:
            us, rss = [], []
            for j in range(GROUP):
                c = q_ * GROUP + j
                rs = _rows(c)
                xj = x[q_][_rows(j), :]
                wr = jnp.concatenate([xj[:, 0:HEAD_DIM], r_pos[rs, sl]], axis=0)
                gs = _dot_nt(wr, s_st)
                se = _dot(s_st, eg[c][0:HEAD_DIM, :])
                yield
                s_st = s_st * e_pos[(c + 1) * CHUNK - 1:(c + 1) * CHUNK, sl] + se + eg[c][HEAD_DIM:2 * HEAD_DIM, :]
                us.append(gs[0:CHUNK, :] + xj[:, HEAD_DIM:2 * HEAD_DIM])
                rss.append(gs[CHUNK:2 * CHUNK, :])
            uvg = jnp.concatenate(us + [vg[q_]], axis=0)
            pre_ref[_grows(q_), sl] = jnp.concatenate(rss, axis=0) + _dot(bot[q_], uvg)
        s_scr[h] = s_st

    return [head(h) for h in range(W_HEADS)], (g, bonus_v)


def _rwkv_finish(pre, g, bonus_v, ln_g, ln_b, bd_bf):
    return (_heads_ln(pre, RWKV_LN_EPS, bd_bf) * ln_g + ln_b + bonus_v) * g


def _run_interleaved(gens):
    pending = list(gens)
    while pending:
        for g in list(pending):
            try:
                next(g)
            except StopIteration:
                pending.remove(g)


def _mix_kernel(pm_ref, pr_ref, pw_ref, gt_ref, qdec_ref, kdec_ref, dmat_ref,
                cdec_ref, gbias_ref, bi_col_ref, bf_col_ref, mln_ref,
                w0_ref, wup_ref, a0_ref, aup_ref, gup_ref, kk_ref, ka_ref, rk_ref,
                wlng_ref, wlnb_ref,
                y_ref,
                cn_scr, m_scr, r_scr, s_scr, prem_scr, prer_scr, prew_scr,
                *, nch):
    @pl.when(pl.program_id(1) == 0)
    def _():
        cn_scr[...] = jnp.zeros(cn_scr.shape, F32)
        m_scr[...] = jnp.full(m_scr.shape, M_INIT, F32)
        r_scr[...] = jnp.zeros(r_scr.shape, F32)
        s_scr[...] = jnp.zeros(s_scr.shape, F32)

    rg_n = GROUP * CHUNK
    row = lax.broadcasted_iota(jnp.int32, (CHUNK, CHUNK), 0)
    col = lax.broadcasted_iota(jnp.int32, (CHUNK, CHUNK), 1)
    tril_bf = jnp.where(col <= row, 1.0, 0.0).astype(BF16)
    rowb = lax.broadcasted_iota(jnp.int32, (BD_W, BD_W), 0)
    colb = lax.broadcasted_iota(jnp.int32, (BD_W, BD_W), 1)
    bd_bf = jnp.where((rowb >> 6) == (colb >> 6), 1.0, 0.0).astype(BF16)
    row2 = lax.broadcasted_iota(jnp.int32, (rg_n, 2 * rg_n), 0)
    col2 = lax.broadcasted_iota(jnp.int32, (rg_n, 2 * rg_n), 1) & (rg_n - 1)
    start2 = (row2 >> 6) << 6
    lower_incl2 = (col2 <= row2) & (col2 >= start2)
    lower_strict2 = (col2 < row2) & (col2 >= start2)
    causal_g = lower_incl2[:, 0:rg_n]
    rowg = lax.broadcasted_iota(jnp.int32, (rg_n, rg_n), 0)
    colg = lax.broadcasted_iota(jnp.int32, (rg_n, rg_n), 1)
    triu_g_bf = jnp.where((rowg <= colg) & (rowg >= ((colg >> 6) << 6)), 1.0, 0.0).astype(BF16)
    eye_g = jnp.where(rowg == colg, 1.0, 0.0).astype(F32)
    sub_masks = []
    for k in range(6):
        same_pair = ((rowg ^ colg) >> (k + 1)) == 0
        sub_masks.append(same_pair & (((rowg >> k) & 1) == 1) & (((colg >> k) & 1) == 0))

    w_gens, (w_gate, bonus_v) = _rwkv_heads(
        pw_ref[0], w0_ref[...], wup_ref[...], a0_ref[...], aup_ref[...],
        gup_ref[...], kk_ref[...], ka_ref[...], rk_ref[...], s_scr, prew_scr, tril_bf, bd_bf,
        lower_incl2, lower_strict2, eye_g, sub_masks, nch)
    pm = pm_ref[0]
    m_gens = _mlstm_heads(pm, gt_ref[...], gbias_ref[...], bi_col_ref[...], bf_col_ref[...],
                          cn_scr, m_scr, prem_scr, tril_bf, triu_g_bf, causal_g, nch)
    pr = pr_ref[0]
    r_gens = _retention_heads(pr, dmat_ref, qdec_ref[...], kdec_ref[...], cdec_ref[...],
                              r_scr, prer_scr, nch)
    _run_interleaved(w_gens + m_gens + r_gens)

    y_ref[0, :, 0:M_WIDTH] = _mlstm_finish(pm, prem_scr[...], mln_ref[...], bd_bf).astype(y_ref.dtype)
    y_ref[0, :, M_WIDTH:M_WIDTH + R_WIDTH] = _retention_finish(pr, prer_scr[...], bd_bf).astype(y_ref.dtype)
    y_ref[0, :, M_WIDTH + R_WIDTH:M_WIDTH + R_WIDTH + W_WIDTH] = _rwkv_finish(
        prew_scr[...], w_gate, bonus_v, wlng_ref[...], wlnb_ref[...], bd_bf).astype(y_ref.dtype)


def _full(shape):
    nd = len(shape)
    return pl.BlockSpec(shape, lambda b, c: (0,) * nd)


def _mixers(pm, pr, pw, gt, tabs, layer_params, l, nch):
    bsz, t, _ = pm.shape
    tt = nch * CHUNK
    nblk = t // tt
    tab_args = list(tabs)
    args = [pm, pr, pw, gt] + tab_args + list(layer_params)
    in_specs = [
        pl.BlockSpec((1, tt, PM_W), lambda b, c: (b, c, 0)),
        pl.BlockSpec((1, tt, R_COLS), lambda b, c: (b, c, 0)),
        pl.BlockSpec((1, tt, W_COLS), lambda b, c: (b, c, 0)),
        pl.BlockSpec((GT_ROWS, tt), lambda b, c: (0, b * nblk + c)),
    ] + [pl.BlockSpec((tt, R_WIDTH), lambda b, c: (c, 0)) for _ in range(2)] + [
        _full(a.shape) for a in tab_args[2:]] + [_layer_spec(a, l, 2) for a in layer_params]
    d_out = M_WIDTH + R_WIDTH + W_WIDTH
    return pl.pallas_call(
        functools.partial(_mix_kernel, nch=nch),
        grid=(bsz, t // tt),
        in_specs=in_specs,
        out_specs=pl.BlockSpec((1, tt, d_out), lambda b, c: (b, c, 0)),
        out_shape=jax.ShapeDtypeStruct((bsz, t, d_out), BF16),
        scratch_shapes=[
            pltpu.VMEM((M_HEADS, HEAD_DIM, 2 * HEAD_DIM), F32),
            pltpu.VMEM((1, 128), F32),
            pltpu.VMEM((R_HEADS, HEAD_DIM, HEAD_DIM), F32),
            pltpu.VMEM((W_HEADS, HEAD_DIM, HEAD_DIM), F32),
            pltpu.VMEM((tt, M_WIDTH), F32),
            pltpu.VMEM((tt, R_WIDTH), F32),
            pltpu.VMEM((tt, W_WIDTH), F32),
        ],
        compiler_params=pltpu.CompilerParams(dimension_semantics=("arbitrary", "arbitrary"),
                                             vmem_limit_bytes=VMEM_LIMIT),
        name="mixers",
    )(*args)


def _ffn_kernel(y_ref, wo_ref, x_ref, g_ref, wg_ref, wu_ref, wd_ref, gf_ref, o_ref, *, ff_chunk, final_norm):
    x = x_ref[...] + jnp.dot(y_ref[...], wo_ref[...], preferred_element_type=F32)
    hb = _bf(_rms(x, g_ref[...]))
    acc = x
    d_ff = wg_ref.shape[1]
    for c in range(d_ff // ff_chunk):
        cs = slice(c * ff_chunk, (c + 1) * ff_chunk)
        gate = jnp.dot(hb, wg_ref[:, cs], preferred_element_type=F32)
        up = jnp.dot(hb, wu_ref[:, cs], preferred_element_type=F32)
        acc = acc + jnp.dot(_bf(_silu(gate) * up), wd_ref[cs, :], preferred_element_type=F32)
    if final_norm:
        acc = _rms(acc, gf_ref[...])
    o_ref[...] = acc


def _out_ffn(y2, x2, wo_all, g_all, wg_all, wu_all, wd_all, gfin, l, tm, final_norm):
    n, d = x2.shape
    kern = functools.partial(_ffn_kernel, ff_chunk=FF_CHUNK, final_norm=final_norm)
    return pl.pallas_call(
        kern,
        grid=(n // tm,),
        in_specs=[pl.BlockSpec((tm, y2.shape[1]), lambda i: (i, 0)),
                  _layer_spec(wo_all, l, 1),
                  pl.BlockSpec((tm, d), lambda i: (i, 0)),
                  _layer_spec(g_all, l, 1),
                  _layer_spec(wg_all, l, 1),
                  _layer_spec(wu_all, l, 1),
                  _layer_spec(wd_all, l, 1),
                  pl.BlockSpec((1, d), lambda i: (0, 0))],
        out_specs=pl.BlockSpec((tm, d), lambda i: (i, 0)),
        out_shape=jax.ShapeDtypeStruct((n, d), F32),
        compiler_params=pltpu.CompilerParams(dimension_semantics=("arbitrary",),
                                             vmem_limit_bytes=VMEM_LIMIT),
        name="out_ffn",
    )(y2, wo_all, x2, g_all, wg_all, wu_all, wd_all, gfin)


def _retention_tables(t):
    h, d = R_HEADS, HEAD_DIM
    pos = jnp.arange(t, dtype=F32)
    theta = 1.0 / (ROPE_BASE ** jnp.linspace(0.0, 1.0, d // 2, dtype=F32))
    ang = pos[:, None] * theta[None, :]
    cos, sin = jnp.cos(ang), jnp.sin(ang)
    cos_t = jnp.tile(jnp.concatenate([cos, cos], axis=1), (1, h))
    sin_t = jnp.tile(jnp.concatenate([-sin, sin], axis=1), (1, h))
    log_g = jnp.log(1.0 - 2.0 ** (-5.0 - jnp.arange(h, dtype=F32)))
    idx = jnp.arange(CHUNK, dtype=F32)
    diff = idx[:, None] - idx[None, :]
    dmat = jnp.where(diff >= 0, jnp.exp(jnp.maximum(diff, 0.0)[None] * log_g[:, None, None]), 0.0)
    q_dec = jnp.exp((idx[None, :] + 1.0) * log_g[:, None])
    k_dec = jnp.exp((CHUNK - 1.0 - idx[None, :]) * log_g[:, None])
    c_dec = jnp.exp(CHUNK * log_g)
    expand = lambda z: jnp.repeat(z.T, d, axis=1)
    eye_g = jnp.eye(GROUP, dtype=F32)
    dmat_g = jnp.einsum('ab,hij->haibj', eye_g, dmat).reshape(h, GROUP * CHUNK, GROUP * CHUNK)
    reps = t // CHUNK
    return (cos_t, sin_t, jnp.tile(expand(q_dec), (reps, 1)), jnp.tile(expand(k_dec), (reps, 1)),
            dmat_g, jnp.repeat(c_dec, d)[None, :])


def kernel(x, w_in, ln1_g, ln2_g, lnf_g, m_conv, m_b_i, m_b_f, m_ln_g, rw_mu, rw_w0, rw_w_up,
           rw_a0, rw_a_up, rw_g_up, rw_k_k, rw_k_a, rw_r_k, rw_ln_g, rw_ln_b, w_out, w_gate,
           w_up, w_down):
    bsz, t, d = x.shape
    depth = w_in.shape[0]
    n = bsz * t
    nc = t // CHUNK
    tm = min(512, n)
    nch = math.gcd(nc, NCH)
    tabs = _retention_tables(t)
    x2 = x.reshape(n, d)
    rows3 = lambda z: z.reshape(depth, 1, -1)
    cols3 = lambda z: z.reshape(depth, -1, 1)
    w_pad = jnp.zeros((depth, d, P_W), BF16)
    w_pad = w_pad.at[:, :, 0:M_COLS].set(w_in[:, :, 0:M_COLS].astype(BF16))
    w_pad = w_pad.at[:, :, PM_W:P_W].set(w_in[:, :, M_COLS:].astype(BF16))
    gate_bias = jnp.concatenate(
        [m_b_i, m_b_f, jnp.zeros((depth, 128 - 2 * M_HEADS), F32)], axis=1)[:, None, :]
    layer_params = (gate_bias, cols3(m_b_i), cols3(m_b_f), rows3(m_ln_g),
                    rows3(rw_w0), rw_w_up, rows3(rw_a0), rw_a_up, rw_g_up, rows3(rw_k_k),
                    rows3(rw_k_a), rows3(rw_r_k), rows3(rw_ln_g), rows3(rw_ln_b))
    wo_bf, wg_bf, wu_bf, wd_bf = (w.astype(BF16) for w in (w_out, w_gate, w_up, w_down))
    ln1, ln2 = rows3(ln1_g), rows3(ln2_g)
    for l in range(depth):
        pm, pr, pw, gt = _in_proj(x2, ln1, w_pad, m_conv, rows3(rw_mu), tabs[0], tabs[1], l, t,
                                  math.gcd(t, tm))
        y = _mixers(pm.reshape(bsz, t, PM_W), pr.reshape(bsz, t, R_COLS), pw.reshape(bsz, t, W_COLS),
                    gt, tabs[2:], layer_params, l, nch)
        x2 = _out_ffn(y.reshape(n, d), x2, wo_bf, ln2, wg_bf, wu_bf, wd_bf, lnf_g.reshape(1, -1), l, tm,
                      final_norm=(l == depth - 1))
    return x2.reshape(bsz, t, d)
```
